```python
import jax, jax.numpy as jnp
from jax import lax
import numpy as np

D_MODEL = 1024
BATCH = 4
SEQ = 4096
DEPTH = 4

GRID_W = 64
CTX_LEN = 256
N_MIXERS = 3
N_LAYERS_A = (DEPTH + 2) // 3
N_LAYERS_B = (DEPTH + 1) // 3
N_LAYERS_C = DEPTH // 3
NORM_EPS = 1e-6

MLSTM_HEADS = 8
MLSTM_DQK = D_MODEL // MLSTM_HEADS // 2
MLSTM_DV = D_MODEL // MLSTM_HEADS
MLSTM_CONV = 3
MLSTM_CHUNK = 64
MLSTM_FGATE_BIAS = 3.0
MLSTM_PROJ = 2 * MLSTM_HEADS * MLSTM_DQK + 2 * D_MODEL + 4 * MLSTM_HEADS

ATTN_HEADS = 8
ATTN_KV_HEADS = 4
ATTN_HEAD_DIM = D_MODEL // ATTN_HEADS
ATTN_BLOCK = 128
ROPE_THETA = 10000.0
ATTN_PROJ = (ATTN_HEADS + 2 * ATTN_KV_HEADS) * ATTN_HEAD_DIM

GLA_HEADS = 4
GLA_DK = D_MODEL // 2 // GLA_HEADS
GLA_DV = D_MODEL // GLA_HEADS
GLA_GATE_RANK = 16
GLA_TAU = 16.0
GLA_CHUNK = 64
GLA_GATE_BIAS = 2.0
GLA_PROJ = 2 * GLA_HEADS * GLA_DK + 2 * D_MODEL + 2 * GLA_GATE_RANK

N_EXPERTS = 32
TOP_K = 4
D_EXPERT = D_MODEL
SWIGLU_LIMIT = 7.0
SWIGLU_ALPHA = 1.702
MOE_BLOCK = 128

kernel_name = "hybrid_mlstm_gqa_gla_moe_dit"

F32 = jnp.float32


def rms_norm(x, g):
    x32 = x.astype(F32)
    y = x32 * lax.rsqrt(jnp.mean(x32 * x32, axis=-1, keepdims=True) + NORM_EPS)
    return (y * g.astype(F32)).astype(x.dtype)


def modulate(x, shift, scale):
    return x * (1 + scale[:, None, :]) + shift[:, None, :]


def depthwise_conv_centred(x, w):
    return lax.conv_general_dilated(
        x, w[:, None, :].astype(x.dtype), window_strides=(1,),
        padding=[((MLSTM_CONV - 1) // 2, MLSTM_CONV // 2)],
        dimension_numbers=("NWC", "WIO", "NWC"), feature_group_count=x.shape[-1])


def to_heads(a, n_heads, d):
    b, n = a.shape[:2]
    return a.reshape(b, n, n_heads, d).transpose(0, 2, 1, 3).astype(F32)


def to_chunks(a, size):
    b, h, n = a.shape[:3]
    return jnp.moveaxis(a.reshape(b, h, n // size, size, *a.shape[3:]), 2, 0)


def from_chunks(y):
    nc, b, h, size = y.shape[:4]
    return jnp.moveaxis(y, 0, 2).reshape(b, h, nc * size, *y.shape[4:])


def bidirectional_scan(scan_fn, ctx_f, ctx_b, lat_f, lat_b, state0):
    flip = lambda t: tuple(jnp.flip(a, axis=2) for a in t)
    h_cf, s_f = scan_fn(ctx_f, state0)
    h_cb, s_b = scan_fn(flip(ctx_b), state0)
    h_lf, _ = scan_fn(lat_f, s_f)
    h_lb, _ = scan_fn(flip(lat_b), s_b)
    return h_cf + jnp.flip(h_cb, 2), h_lf + jnp.flip(h_lb, 2)


def mlstm_chunk_step(carry, xs):
    C, n, m = carry
    q, k, v, li, lf = xs
    size = q.shape[2]
    tri = jnp.tril(jnp.ones((size, size), bool))
    b = jnp.cumsum(lf, axis=-1)
    dmat = jnp.where(tri, b[..., :, None] - b[..., None, :] + li[..., None, :], -jnp.inf)
    inter = b + m[..., None]
    m_t = jnp.maximum(inter, jnp.max(dmat, axis=-1))
    w = jnp.exp(dmat - m_t[..., None])
    a_inter = jnp.exp(inter - m_t)
    s = jnp.einsum("bhtd,bhsd->bhts", q, k) * w
    num = a_inter[..., None] * jnp.einsum("bhtd,bhvd->bhtv", q, C) + jnp.einsum("bhts,bhsv->bhtv", s, v)
    den = a_inter * jnp.einsum("bhtd,bhd->bht", q, n) + jnp.sum(s, axis=-1)
    h = num / jnp.maximum(jnp.abs(den), jnp.exp(-m_t))[..., None]
    b_last = b[..., -1]
    g = b_last[..., None] - b + li
    m_new = jnp.maximum(b_last + m, jnp.max(g, axis=-1))
    decay = jnp.exp(b_last + m - m_new)
    wg = jnp.exp(g - m_new[..., None])
    C_new = decay[..., None, None] * C + jnp.einsum("bhs,bhsv,bhsd->bhvd", wg, v, k)
    n_new = decay[..., None] * n + jnp.einsum("bhs,bhsd->bhd", wg, k)
    return (C_new, n_new, m_new), h


def mlstm_scan(inputs, state):
    xs = tuple(to_chunks(a, MLSTM_CHUNK) for a in inputs)
    state, h = lax.scan(mlstm_chunk_step, state, xs)
    return from_chunks(h), state


def mlstm_mixer(u_ctx, u_lat, w_in, conv_w, gate_b, norm_g, w_out, need_ctx):
    H, dk, dv, D = MLSTM_HEADS, MLSTM_DQK, MLSTM_DV, D_MODEL

    def project(u):
        p = u @ w_in
        b_, n_ = u.shape[:2]
        qk = jax.nn.silu(depthwise_conv_centred(p[..., :2 * H * dk], conv_w))
        q = to_heads(qk[..., :H * dk], H, dk) * dk ** -0.5
        k = to_heads(qk[..., H * dk:], H, dk)
        v = to_heads(p[..., 2 * H * dk:2 * H * dk + D], H, dv)
        o = p[..., 2 * H * dk + D:2 * H * dk + 2 * D]
        gates = p[..., 2 * H * dk + 2 * D:].reshape(b_, n_, 4, H).astype(F32) + gate_b
        gates = gates.transpose(2, 0, 3, 1)
        fwd = (q, k, v, gates[0], jax.nn.log_sigmoid(gates[1]))
        bwd = (q, k, v, gates[2], jax.nn.log_sigmoid(gates[3]))
        return fwd, bwd, o

    cf, cb, o_ctx = project(u_ctx)
    lf, lb, o_lat = project(u_lat)
    b_ = u_lat.shape[0]
    state0 = (jnp.zeros((b_, H, dv, dk), F32), jnp.zeros((b_, H, dk), F32), jnp.zeros((b_, H), F32))
    h_ctx, h_lat = bidirectional_scan(mlstm_scan, cf, cb, lf, lb, state0)

    def readout(h, o):
        hh = rms_norm(jnp.swapaxes(h, 1, 2), norm_g.reshape(H, dv))
        hh = hh.reshape(hh.shape[0], hh.shape[1], D) * jax.nn.sigmoid(o.astype(F32))
        return hh.astype(o.dtype) @ w_out

    return (readout(h_ctx, o_ctx) if need_ctx else None), readout(h_lat, o_lat)


def rope_axis(x, pos):
    half = x.shape[-1] // 2
    inv = ROPE_THETA ** (-jnp.arange(half, dtype=F32) / half)
    ang = pos[:, None] * inv
    cos, sin = jnp.cos(ang)[:, None, :], jnp.sin(ang)[:, None, :]
    x1, x2 = x[..., :half], x[..., half:]
    return jnp.concatenate([x1 * cos - x2 * sin, x1 * sin + x2 * cos], axis=-1)


def rope_2d(x, rows, cols):
    half = x.shape[-1] // 2
    x32 = x.astype(F32)
    out = jnp.concatenate([rope_axis(x32[..., :half], rows), rope_axis(x32[..., half:], cols)], axis=-1)
    return out.astype(x.dtype)


def attend(qb, keys, vals):
    s = jnp.einsum("bqkgd,bnkd->bkgqn", qb, keys).astype(F32) * ATTN_HEAD_DIM ** -0.5
    p = jax.nn.softmax(s, axis=-1).astype(vals.dtype)
    return jnp.einsum("bkgqn,bnkd->bqkgd", p, vals)


def gqa_mixer(u_ctx, u_lat, w_in, q_norm_g, k_norm_g, w_out, need_ctx):
    Hq, Hkv, Dh, D = ATTN_HEADS, ATTN_KV_HEADS, ATTN_HEAD_DIM, D_MODEL
    G = Hq // Hkv

    def project(u):
        p = u @ w_in
        b_, n_ = u.shape[:2]
        q = rms_norm(p[..., :Hq * Dh].reshape(b_, n_, Hq, Dh), q_norm_g)
        k = rms_norm(p[..., Hq * Dh:(Hq + Hkv) * Dh].reshape(b_, n_, Hkv, Dh), k_norm_g)
        v = p[..., (Hq + Hkv) * Dh:].reshape(b_, n_, Hkv, Dh)
        return q, k, v

    qc, kc, vc = project(u_ctx)
    ql, kl, vl = project(u_lat)
    b_, n_lat = u_lat.shape[:2]
    n_rows = n_lat // GRID_W
    rows = jnp.repeat(jnp.arange(n_rows, dtype=F32), GRID_W)
    cols = jnp.tile(jnp.arange(GRID_W, dtype=F32), n_rows)
    ql = rope_2d(ql, rows, cols)
    kl = rope_2d(kl, rows, cols)
    keys = jnp.concatenate([kc, kl], axis=1)
    vals = jnp.concatenate([vc, vl], axis=1)

    n_blk = n_lat // ATTN_BLOCK
    qlb = ql.reshape(b_, n_blk, ATTN_BLOCK, Hkv, G, Dh).transpose(1, 0, 2, 3, 4, 5)
    out = lax.map(lambda qb: attend(qb, keys, vals), qlb)
    y_lat = out.transpose(1, 0, 2, 3, 4, 5).reshape(b_, n_lat, D) @ w_out
    y_ctx = None
    if need_ctx:
        n_ctx = u_ctx.shape[1]
        y_ctx = attend(qc.reshape(b_, n_ctx, Hkv, G, Dh), kc, vc).reshape(b_, n_ctx, D) @ w_out
    return y_ctx, y_lat


def gla_chunk_step(S, xs):
    q, k, v, la = xs
    size = q.shape[2]
    tri = jnp.tril(jnp.ones((size, size), bool))
    b = jnp.cumsum(la, axis=2)
    o_inter = jnp.einsum("bhtd,bhdv->bhtv", q * jnp.exp(b), S)
    dec = jnp.exp(jnp.where(tri[:, :, None], b[:, :, :, None, :] - b[:, :, None, :, :], -jnp.inf))
    a = jnp.einsum("bhtd,bhtsd,bhsd->bhts", q, dec, k)
    o = o_inter + jnp.einsum("bhts,bhsv->bhtv", a, v)
    b_last = b[:, :, -1]
    S_new = jnp.exp(b_last)[..., None] * S + jnp.einsum(
        "bhsd,bhsv->bhdv", k * jnp.exp(b_last[:, :, None, :] - b), v)
    return S_new, o


def gla_scan(inputs, S):
    xs = tuple(to_chunks(a, GLA_CHUNK) for a in inputs)
    S, o = lax.scan(gla_chunk_step, S, xs)
    return from_chunks(o), S


def gla_mixer(u_ctx, u_lat, w_in, alpha_w2, alpha_b, norm_g, w_out, need_ctx):
    H, dk, dv, R, D = GLA_HEADS, GLA_DK, GLA_DV, GLA_GATE_RANK, D_MODEL

    def project(u):
        p = u @ w_in
        b_, n_ = u.shape[:2]
        q = to_heads(p[..., :H * dk], H, dk) * dk ** -0.5
        k = to_heads(p[..., H * dk:2 * H * dk], H, dk)
        v = to_heads(p[..., 2 * H * dk:2 * H * dk + D], H, dv)
        g = p[..., 2 * H * dk + D:2 * H * dk + 2 * D]
        a_low = p[..., 2 * H * dk + 2 * D:].reshape(b_, n_, 2, R)
        la = jax.nn.log_sigmoid(jnp.einsum("bnjr,jrc->bnjc", a_low, alpha_w2).astype(F32) + alpha_b) / GLA_TAU
        fwd = (q, k, v, to_heads(la[:, :, 0], H, dk))
        bwd = (q, k, v, to_heads(la[:, :, 1], H, dk))
        return fwd, bwd, g

    cf, cb, g_ctx = project(u_ctx)
    lf, lb, g_lat = project(u_lat)
    state0 = jnp.zeros((u_lat.shape[0], H, dk, dv), F32)
    h_ctx, h_lat = bidirectional_scan(gla_scan, cf, cb, lf, lb, state0)

    def readout(h, g):
        hh = rms_norm(jnp.swapaxes(h, 1, 2), norm_g.reshape(H, dv))
        hh = hh.reshape(hh.shape[0], hh.shape[1], D) * jax.nn.silu(g.astype(F32))
        return hh.astype(g.dtype) @ w_out

    return (readout(h_ctx, g_ctx) if need_ctx else None), readout(h_lat, g_lat)


def moe(x, w_r, b_r, w_gu, b_gu, w_d, b_d):
    T, D = x.shape
    logits = (x @ w_r).astype(F32) + b_r
    top_logit, top_e = lax.top_k(logits, TOP_K)
    gate = jax.nn.softmax(top_logit, axis=-1)
    n_assign = T * TOP_K
    flat_e = top_e.reshape(-1)
    flat_t = jnp.repeat(jnp.arange(T, dtype=jnp.int32), TOP_K)
    flat_g = gate.reshape(-1)
    order = jnp.argsort(flat_e)
    e_sorted = flat_e[order]
    counts = jnp.bincount(flat_e, length=N_EXPERTS)
    starts = jnp.cumsum(counts) - counts
    padded = (counts + MOE_BLOCK - 1) // MOE_BLOCK * MOE_BLOCK
    pad_ends = jnp.cumsum(padded)
    pad_starts = pad_ends - padded
    dest = pad_starts[e_sorted] + jnp.arange(n_assign, dtype=jnp.int32) - starts[e_sorted]
    cap = (-(-n_assign // MOE_BLOCK) + N_EXPERTS) * MOE_BLOCK
    tok_buf = jnp.full((cap,), T, jnp.int32).at[dest].set(flat_t[order])
    gate_buf = jnp.zeros((cap,), x.dtype).at[dest].set(flat_g[order].astype(x.dtype))
    n_blocks = cap // MOE_BLOCK
    block_e = jnp.minimum(jnp.searchsorted(pad_ends, jnp.arange(n_blocks, dtype=jnp.int32) * MOE_BLOCK,
                                           side="right"), N_EXPERTS - 1)
    x_pad = jnp.concatenate([x, jnp.zeros((1, D), x.dtype)], axis=0)
    xb = x_pad[tok_buf].reshape(n_blocks, MOE_BLOCK, D)

    def expert_block(args):
        xs, e = args
        gu = xs @ w_gu[e] + b_gu[e]
        g = jnp.minimum(gu[:, :D_EXPERT], SWIGLU_LIMIT)
        lin = jnp.clip(gu[:, D_EXPERT:], -SWIGLU_LIMIT, SWIGLU_LIMIT)
        hdn = g * jax.nn.sigmoid(SWIGLU_ALPHA * g) * (lin + 1)
        return hdn @ w_d[e] + b_d[e]

    yb = lax.map(expert_block, (xb, block_e)).reshape(cap, D)
    out = jnp.zeros((T + 1, D), x.dtype).at[tok_buf].add(yb * gate_buf[:, None])
    return out[:T]


def setup_inputs(seed: int = 0) -> dict:
    key = jax.random.key(seed)
    ks = iter(jax.random.split(key, 40))
    nrm = lambda shape, scale: jax.random.normal(next(ks), shape, jnp.float32) * scale
    D = D_MODEL
    d_in = D ** -0.5
    fgate_off = jnp.array([0.0, MLSTM_FGATE_BIAS, 0.0, MLSTM_FGATE_BIAS], jnp.float32)[None, :, None]
    return {
        "x": nrm((BATCH, SEQ, D), 1.0),
        "c": nrm((BATCH, D), 1.0),
        "ctx": nrm((BATCH, CTX_LEN, D), 1.0),
        "c_ctx": nrm((D,), 1.0),
        "norm1_g": 1.0 + nrm((DEPTH, D), 0.02),
        "norm2_g": 1.0 + nrm((DEPTH, D), 0.02),
        "mod_w": nrm((DEPTH, D, 6 * D), 0.5 * d_in),
        "mod_b": nrm((DEPTH, 6 * D), 0.02),
        "mlstm_w_in": nrm((N_LAYERS_A, D, MLSTM_PROJ), d_in),
        "mlstm_conv_w": nrm((N_LAYERS_A, MLSTM_CONV, 2 * MLSTM_HEADS * MLSTM_DQK), MLSTM_CONV ** -0.5),
        "mlstm_gate_b": nrm((N_LAYERS_A, 4, MLSTM_HEADS), 0.1) + fgate_off,
        "mlstm_out_norm_g": 1.0 + nrm((N_LAYERS_A, D), 0.02),
        "mlstm_w_out": nrm((N_LAYERS_A, D, D), d_in),
        "attn_w_in": nrm((N_LAYERS_B, D, ATTN_PROJ), d_in),
        "attn_q_norm_g": 1.0 + nrm((N_LAYERS_B, ATTN_HEAD_DIM), 0.02),
        "attn_k_norm_g": 1.0 + nrm((N_LAYERS_B, ATTN_HEAD_DIM), 0.02),
        "attn_w_out": nrm((N_LAYERS_B, D, D), d_in),
        "gla_w_in": nrm((N_LAYERS_C, D, GLA_PROJ), d_in),
        "gla_alpha_w2": nrm((N_LAYERS_C, 2, GLA_GATE_RANK, GLA_HEADS * GLA_DK), GLA_GATE_RANK ** -0.5),
        "gla_alpha_b": nrm((N_LAYERS_C, 2, GLA_HEADS * GLA_DK), 0.1) + GLA_GATE_BIAS,
        "gla_out_norm_g": 1.0 + nrm((N_LAYERS_C, D), 0.02),
        "gla_w_out": nrm((N_LAYERS_C, D, D), d_in),
        "router_w": nrm((DEPTH, D, N_EXPERTS), d_in),
        "router_b": nrm((DEPTH, N_EXPERTS), 0.01),
        "moe_w_gu": nrm((DEPTH, N_EXPERTS, D, 2 * D_EXPERT), d_in),
        "moe_b_gu": nrm((DEPTH, N_EXPERTS, 2 * D_EXPERT), 0.02),
        "moe_w_down": nrm((DEPTH, N_EXPERTS, D_EXPERT, D), D_EXPERT ** -0.5),
        "moe_b_down": nrm((DEPTH, N_EXPERTS, D), 0.02),
    }


def reference(x, c, ctx, c_ctx, norm1_g, norm2_g, mod_w, mod_b,
              mlstm_w_in, mlstm_conv_w, mlstm_gate_b, mlstm_out_norm_g, mlstm_w_out,
              attn_w_in, attn_q_norm_g, attn_k_norm_g, attn_w_out,
              gla_w_in, gla_alpha_w2, gla_alpha_b, gla_out_norm_g, gla_w_out,
              router_w, router_b, moe_w_gu, moe_b_gu, moe_w_down, moe_b_down):
    h_lat, h_ctx = x, ctx
    cond_lat = jax.nn.silu(c)
    cond_ctx = jax.nn.silu(c_ctx)[None]
    b_, n_lat, D = x.shape
    n_ctx = ctx.shape[1]
    for layer in range(DEPTH):
        last = layer == DEPTH - 1
        mod_l = jnp.split(cond_lat @ mod_w[layer] + mod_b[layer], 6, axis=-1)
        mod_c = jnp.split(cond_ctx @ mod_w[layer] + mod_b[layer], 6, axis=-1)
        u_lat = modulate(rms_norm(h_lat, norm1_g[layer]), mod_l[0], mod_l[1])
        u_ctx = modulate(rms_norm(h_ctx, norm1_g[layer]), mod_c[0], mod_c[1])
        kind, j = layer % N_MIXERS, layer // N_MIXERS
        if kind == 0:
            y_ctx, y_lat = mlstm_mixer(u_ctx, u_lat, mlstm_w_in[j], mlstm_conv_w[j], mlstm_gate_b[j],
                                       mlstm_out_norm_g[j], mlstm_w_out[j], not last)
        elif kind == 1:
            y_ctx, y_lat = gqa_mixer(u_ctx, u_lat, attn_w_in[j], attn_q_norm_g[j], attn_k_norm_g[j],
                                     attn_w_out[j], not last)
        else:
            y_ctx, y_lat = gla_mixer(u_ctx, u_lat, gla_w_in[j], gla_alpha_w2[j], gla_alpha_b[j],
                                     gla_out_norm_g[j], gla_w_out[j], not last)
        h_lat = h_lat + mod_l[2][:, None, :] * y_lat
        v_lat = modulate(rms_norm(h_lat, norm2_g[layer]), mod_l[3], mod_l[4])
        moe_args = (router_w[layer], router_b[layer], moe_w_gu[layer], moe_b_gu[layer],
                    moe_w_down[layer], moe_b_down[layer])
        if last:
            f_lat = moe(v_lat.reshape(-1, D), *moe_args).reshape(b_, n_lat, D)
        else:
            h_ctx = h_ctx + mod_c[2][:, None, :] * y_ctx
            v_ctx = modulate(rms_norm(h_ctx, norm2_g[layer]), mod_c[3], mod_c[4])
            tokens = jnp.concatenate([v_ctx.reshape(-1, D), v_lat.reshape(-1, D)], axis=0)
            f_all = moe(tokens, *moe_args)
            f_ctx = f_all[:b_ * n_ctx].reshape(b_, n_ctx, D)
            f_lat = f_all[b_ * n_ctx:].reshape(b_, n_lat, D)
            h_ctx = h_ctx + mod_c[5][:, None, :] * f_ctx
        h_lat = h_lat + mod_l[5][:, None, :] * f_lat
    return h_lat
```

```python
import functools

import jax
import jax.numpy as jnp
from jax import lax
from jax.experimental import pallas as pl
from jax.experimental.pallas import tpu as pltpu

F32 = jnp.float32
BF16 = jnp.bfloat16
I32 = jnp.int32
HIGHEST = lax.Precision.HIGHEST

D_MODEL = 1024
GRID_W = 64
NORM_EPS = 1e-6

MLSTM_HEADS = 8
MLSTM_DQK = 64
MLSTM_DV = 128
MLSTM_QK = MLSTM_HEADS * MLSTM_DQK

ATTN_HEADS = 8
ATTN_KV_HEADS = 4
ATTN_GROUP = ATTN_HEADS // ATTN_KV_HEADS
ATTN_HEAD_DIM = 128
ROPE_THETA = 10000.0

GLA_HEADS = 4
GLA_DK = 128
GLA_DV = 256
GLA_GATE_RANK = 16
GLA_TAU = 16.0
GLA_QK = GLA_HEADS * GLA_DK

N_EXPERTS = 32
TOP_K = 4
D_EXPERT = D_MODEL
SWIGLU_LIMIT = 7.0
SWIGLU_ALPHA = 1.702

LANES = 128
SUBLANES = 8
TOKEN_TILE = 256
MLSTM_CHUNK = 128
GLA_TILE = 128
GLA_CHUNK = 32
MOE_BLOCK = 256
VMEM_LIMIT = 48 * 1024 * 1024
NEG_BIG = -1e30


def _cparams(semantics, vmem=VMEM_LIMIT):
    return pltpu.CompilerParams(dimension_semantics=semantics, vmem_limit_bytes=vmem)


def _log_sigmoid(x):
    return jnp.minimum(x, 0.0) - jnp.log(1.0 + jnp.exp(-jnp.abs(x)))


def _sigmoid(x):
    return 1.0 / (1.0 + jnp.exp(-x))


def _norm_mod(x, g, shift, scale):
    ms = jnp.mean(x * x, axis=-1, keepdims=True)
    y = x * lax.rsqrt(ms + NORM_EPS) * g
    return y * (1.0 + scale) + shift


def _head_rms(x, head_dim):
    outs = []
    for i in range(x.shape[1] // head_dim):
        xs = x[:, i * head_dim:(i + 1) * head_dim]
        ms = jnp.mean(xs * xs, axis=-1, keepdims=True)
        outs.append(xs * lax.rsqrt(ms + NORM_EPS))
    return jnp.concatenate(outs, axis=1)


def _dot(a, b):
    return jnp.dot(a, b, preferred_element_type=F32)


def _dot_nt(a, b):
    return lax.dot_general(a, b, (((1,), (1,)), ((), ())), preferred_element_type=F32)


def _dot_tn(a, b):
    return lax.dot_general(a, b, (((0,), (0,)), ((), ())), preferred_element_type=F32)


def _mod_kernel(c_ref, w_ref, b_ref, o_ref):
    x = c_ref[...]
    x = x * _sigmoid(x)
    o_ref[0, 0] = _dot(x.astype(BF16), w_ref[0].astype(BF16)) + b_ref[0, 0]


def _modulation(cond_rows, mod_w, mod_b):
    depth, d, _ = mod_w.shape
    rows = cond_rows.shape[0]
    out = pl.pallas_call(
        _mod_kernel,
        grid=(depth, 6),
        in_specs=[
            pl.BlockSpec((rows, d), lambda l, j: (0, 0)),
            pl.BlockSpec((1, d, d), lambda l, j: (l, 0, j)),
            pl.BlockSpec((1, 1, 1, d), lambda l, j: (l, j, 0, 0)),
        ],
        out_specs=pl.BlockSpec((1, 1, rows, d), lambda l, j: (l, j, 0, 0)),
        out_shape=jax.ShapeDtypeStruct((depth, 6, rows, d), F32),
        compiler_params=_cparams(("parallel", "parallel")),
        name="modulation",
    )(cond_rows, mod_w, mod_b.reshape(depth, 6, 1, d))
    return out.transpose(0, 2, 1, 3)


class _Layout:
    def __init__(self, batch, n_ctx, n_lat):
        self.batch = batch
        self.n_ctx = n_ctx
        self.n_lat = n_lat
        self.n = n_ctx + n_lat
        assert n_ctx % TOKEN_TILE == 0 and n_lat % TOKEN_TILE == 0
        assert n_ctx % MLSTM_CHUNK == 0 and n_lat % MLSTM_CHUNK == 0
        assert n_ctx % GLA_TILE == 0 and n_lat % GLA_TILE == 0
        self.tiles = self.n // TOKEN_TILE
        self.ctx_tiles = n_ctx // TOKEN_TILE
        self.ctx_row = batch

    def mod_index(self, b, t):
        return jnp.where(t < self.ctx_tiles, self.ctx_row, b)


def _tok_spec(width, lay):
    return pl.BlockSpec((1, TOKEN_TILE, width), lambda b, t: (b, t, 0))


def _full_spec(shape):
    nd = len(shape)
    return pl.BlockSpec(shape, lambda b, t: (0,) * nd)


def _mod_spec(lay):
    return pl.BlockSpec((1, 6, D_MODEL), lambda b, t: (lay.mod_index(b, t), 0, 0))


def _mlstm_in_kernel(h_ref, g_ref, mod_ref, w_ref, wg_ref, wgt_ref, gb_ref, gbt_ref,
                     qk_ref, v_ref, o_ref, gc_ref, gr_ref):
    mod = mod_ref[0]
    u = _norm_mod(h_ref[0], g_ref[...], mod[0:1], mod[1:2]).astype(BF16)
    p = _dot(u, w_ref[...])
    qk_ref[0] = p[:, :2 * MLSTM_QK]
    v_ref[0] = p[:, 2 * MLSTM_QK:2 * MLSTM_QK + D_MODEL].astype(BF16)
    o_ref[0] = p[:, 2 * MLSTM_QK + D_MODEL:]
    gc = _dot(u, wg_ref[...]) + gb_ref[...]
    lane = lax.broadcasted_iota(I32, gc.shape, 1)
    gc_ref[0] = jnp.where(((lane >> 3) & 1) == 1, _log_sigmoid(gc), gc)
    gr = _dot_nt(wgt_ref[...], u) + gbt_ref[...]
    sub = lax.broadcasted_iota(I32, gr.shape, 0)
    gr_ref[0] = jnp.where(((sub >> 3) & 1) == 1, _log_sigmoid(gr), gr)


def _mlstm_in(h, norm_g, mod_l, w_in, gate_b, lay):
    d = D_MODEL
    n_main = 2 * MLSTM_QK + 2 * d
    n_gate = 4 * MLSTM_HEADS
    w_main = w_in[:, :n_main].astype(BF16)
    w_gate = w_in[:, n_main:]
    wg = jnp.zeros((d, LANES), F32).at[:, :n_gate].set(w_gate).astype(BF16)
    wgt = w_gate.T.astype(BF16)
    gb = jnp.zeros((1, LANES), F32).at[0, :n_gate].set(gate_b.reshape(-1))
    gbt = gate_b.reshape(n_gate, 1)
    b, n = lay.batch, lay.n
    return pl.pallas_call(
        _mlstm_in_kernel,
        grid=(b, lay.tiles),
        in_specs=[
            _tok_spec(d, lay), _full_spec((1, d)), _mod_spec(lay),
            _full_spec((d, n_main)), _full_spec((d, LANES)), _full_spec((n_gate, d)),
            _full_spec((1, LANES)), _full_spec((n_gate, 1)),
        ],
        out_specs=[
            _tok_spec(2 * MLSTM_QK, lay), _tok_spec(d, lay), _tok_spec(d, lay), _tok_spec(LANES, lay),
            pl.BlockSpec((1, n_gate, TOKEN_TILE), lambda bb, t: (bb, 0, t)),
        ],
        out_shape=[
            jax.ShapeDtypeStruct((b, n, 2 * MLSTM_QK), F32),
            jax.ShapeDtypeStruct((b, n, d), BF16),
            jax.ShapeDtypeStruct((b, n, d), F32),
            jax.ShapeDtypeStruct((b, n, LANES), F32),
            jax.ShapeDtypeStruct((b, n_gate, n), F32),
        ],
        compiler_params=_cparams(("parallel", "parallel")),
        name="mlstm_in",
    )(h, norm_g.reshape(1, d), mod_l, w_main, wg, wgt, gb, gbt)


def _scan_chunk_index(reverse, j, n_chunks, ctx_chunks):
    back = jnp.where(j < ctx_chunks, ctx_chunks - 1 - j, n_chunks - 1 - (j - ctx_chunks))
    return jnp.where(reverse, back, j)


def _mlstm_scan_kernel(qk_ref, qkp_ref, qkn_ref, v_ref, gc_ref, gr_ref, cw_ref, out_ref, st_ref, m_ref,
                       *, n_chunks, ctx_chunks):
    L = MLSTM_CHUNK
    dk, dv, nh = MLSTM_DQK, MLSTM_DV, MLSTM_HEADS
    d = pl.program_id(1)
    j = pl.program_id(2)
    c = _scan_chunk_index(d == 1, j, n_chunks, ctx_chunks)

    @pl.when(j == 0)
    def _():
        st_ref[...] = jnp.zeros_like(st_ref)
        m_ref[...] = jnp.zeros_like(m_ref)

    x = qk_ref[0]
    seg_first = (c == 0) | (c == ctx_chunks)
    seg_last = (c == ctx_chunks - 1) | (c == n_chunks - 1)
    prev_row = jnp.where(seg_first, 0.0, qkp_ref[0, SUBLANES - 1:SUBLANES, :])
    next_row = jnp.where(seg_last, 0.0, qkn_ref[0, 0:1, :])
    rid = lax.broadcasted_iota(I32, x.shape, 0)
    x_m1 = jnp.where(rid == 0, prev_row, pltpu.roll(x, 1, 0))
    x_p1 = jnp.where(rid == L - 1, next_row, pltpu.roll(x, L - 1, 0))
    cw = cw_ref[...]
    y = cw[0:1] * x_m1 + cw[1:2] * x + cw[2:3] * x_p1
    qk = y * _sigmoid(y)
    q = (qk[:, :MLSTM_QK] * dk ** -0.5).astype(BF16)
    k_t = qk[:, MLSTM_QK:].T.astype(BF16)
    v = v_ref[0]

    gcol = gc_ref[0]
    gcol = jnp.where(d == 0, gcol, pltpu.roll(gcol, LANES - 2 * nh, 1))
    grow = gr_ref[0, pl.ds(pl.multiple_of(d * 2 * nh, 2 * nh), 2 * nh), :]

    row = lax.broadcasted_iota(I32, (L, L), 0)
    col = lax.broadcasted_iota(I32, (L, L), 1)
    sgn = 1 - 2 * d
    mask = sgn * (row - col) >= 0
    mask_f = mask.astype(F32)
    mask_tf = (sgn * (col - row) >= 0).astype(F32)
    b_col = jnp.dot(mask_f, gcol, precision=HIGHEST, preferred_element_type=F32)
    b_row = jnp.dot(grow, mask_tf, precision=HIGHEST, preferred_element_type=F32)
    b_tot = jnp.sum(gcol, axis=0, keepdims=True)

    ones_col = (lax.broadcasted_iota(I32, (L, LANES), 1) == 0).astype(BF16)
    for h in range(nh):
        bc = b_col[:, nh + h:nh + h + 1]
        br = b_row[nh + h:nh + h + 1, :]
        li_r = grow[h:h + 1, :]
        li_c = gcol[:, h:h + 1]
        m_prev = m_ref[h, 0:1, 0:1]
        dmat = jnp.where(mask, bc - br + li_r, -jnp.inf)
        inter = bc + m_prev
        m_t = jnp.maximum(inter, jnp.max(dmat, axis=-1, keepdims=True))
        w = jnp.exp(dmat - m_t)
        a_inter = jnp.exp(inter - m_t)
        q_h = q[:, h * dk:(h + 1) * dk]
        kt_h = k_t[h * dk:(h + 1) * dk, :]
        s = _dot(q_h, kt_h) * w
        v_aug = jnp.concatenate([v[:, h * dv:(h + 1) * dv], ones_col], axis=1)
        st = st_ref[h]
        r = a_inter * _dot(q_h, st.astype(BF16)) + _dot(s.astype(BF16), v_aug)
        num = r[:, :dv]
        den = r[:, dv:dv + 1]
        out_ref[0, 0, :, h * dv:(h + 1) * dv] = num / jnp.maximum(jnp.abs(den), jnp.exp(-m_t))
        b_last = b_tot[:, nh + h:nh + h + 1]
        g = b_last - bc + li_c
        m_new = jnp.maximum(b_last + m_prev, jnp.max(g, axis=0, keepdims=True))
        decay = jnp.exp(b_last + m_prev - m_new)
        wg = jnp.exp(g - m_new)
        st_ref[h] = decay * st + _dot(kt_h, (wg * v_aug.astype(F32)).astype(BF16))
        m_ref[h] = jnp.broadcast_to(m_new, (SUBLANES, LANES))


def _mlstm_scan(qk_pre, v, gcol, grow, conv_w, lay):
    b, n = lay.batch, lay.n
    L = MLSTM_CHUNK
    n_chunks, ctx_chunks = n // L, lay.n_ctx // L
    halo = L // SUBLANES
    n_halo = n // SUBLANES

    def cidx(d, j):
        return _scan_chunk_index(d == 1, j, n_chunks, ctx_chunks)

    kern = functools.partial(_mlstm_scan_kernel, n_chunks=n_chunks, ctx_chunks=ctx_chunks)
    return pl.pallas_call(
        kern,
        grid=(b, 2, n_chunks),
        in_specs=[
            pl.BlockSpec((1, L, 2 * MLSTM_QK), lambda bb, d, j: (bb, cidx(d, j), 0)),
            pl.BlockSpec((1, SUBLANES, 2 * MLSTM_QK),
                         lambda bb, d, j: (bb, jnp.maximum(cidx(d, j) * halo - 1, 0), 0)),
            pl.BlockSpec((1, SUBLANES, 2 * MLSTM_QK),
                         lambda bb, d, j: (bb, jnp.minimum((cidx(d, j) + 1) * halo, n_halo - 1), 0)),
            pl.BlockSpec((1, L, D_MODEL), lambda bb, d, j: (bb, cidx(d, j), 0)),
            pl.BlockSpec((1, L, LANES), lambda bb, d, j: (bb, cidx(d, j), 0)),
            pl.BlockSpec((1, 4 * MLSTM_HEADS, L), lambda bb, d, j: (bb, 0, cidx(d, j))),
            pl.BlockSpec((3, 2 * MLSTM_QK), lambda bb, d, j: (0, 0)),
        ],
        out_specs=pl.BlockSpec((1, 1, L, D_MODEL), lambda bb, d, j: (d, bb, cidx(d, j), 0)),
        out_shape=jax.ShapeDtypeStruct((2, b, n, D_MODEL), F32),
        scratch_shapes=[
            pltpu.VMEM((MLSTM_HEADS, MLSTM_DQK, MLSTM_DV + LANES), F32),
            pltpu.VMEM((MLSTM_HEADS, SUBLANES, LANES), F32),
        ],
        compiler_params=_cparams(("parallel", "parallel", "arbitrary")),
        name="mlstm_scan",
    )(qk_pre, qk_pre, qk_pre, v, gcol, grow, conv_w)


def _gqa_in_kernel(h_ref, g_ref, mod_ref, w_ref, qg_ref, kg_ref, cos_ref, sin_ref, q_ref, k_ref, v_ref):
    hd = ATTN_HEAD_DIM
    mod = mod_ref[0]
    u = _norm_mod(h_ref[0], g_ref[...], mod[0:1], mod[1:2]).astype(BF16)
    p = _dot(u, w_ref[...])
    cos = cos_ref[...]
    sin = sin_ref[...]
    lane = lax.broadcasted_iota(I32, cos.shape, 1)
    first_half = (lane & (hd // 4)) == 0

    def rope(xh):
        swapped = jnp.where(first_half, pltpu.roll(xh, hd - hd // 4, 1), pltpu.roll(xh, hd // 4, 1))
        return xh * cos + swapped * sin

    nq, nk = ATTN_HEADS * hd, ATTN_KV_HEADS * hd
    qn = _head_rms(p[:, :nq], hd)
    kn = _head_rms(p[:, nq:nq + nk], hd)
    qg = qg_ref[...]
    kg = kg_ref[...]
    for i in range(ATTN_HEADS):
        q_ref[0, :, i * hd:(i + 1) * hd] = (rope(qn[:, i * hd:(i + 1) * hd] * qg) * hd ** -0.5).astype(BF16)
    for i in range(ATTN_KV_HEADS):
        k_ref[0, :, i * hd:(i + 1) * hd] = rope(kn[:, i * hd:(i + 1) * hd] * kg).astype(BF16)
    v_ref[0] = p[:, nq + nk:].astype(BF16)


def _rope_tables(lay):
    hd = ATTN_HEAD_DIM
    quarter = hd // 4
    inv = ROPE_THETA ** (-jnp.arange(quarter, dtype=F32) / quarter)
    pos = jnp.arange(lay.n_lat)
    rows = (pos // GRID_W).astype(F32)
    cols = (pos % GRID_W).astype(F32)
    ang = jnp.concatenate([jnp.tile(rows[:, None] * inv, (1, 2)), jnp.tile(cols[:, None] * inv, (1, 2))], axis=1)
    sign = jnp.tile(jnp.concatenate([-jnp.ones(quarter, F32), jnp.ones(quarter, F32)]), 2)
    cos = jnp.concatenate([jnp.ones((lay.n_ctx, hd), F32), jnp.cos(ang)], axis=0)
    sin = jnp.concatenate([jnp.zeros((lay.n_ctx, hd), F32), jnp.sin(ang) * sign], axis=0)
    return cos, sin


def _gqa_in(h, norm_g, mod_l, w_in, q_g, k_g, lay):
    d, hd = D_MODEL, ATTN_HEAD_DIM
    nq, nk = ATTN_HEADS * hd, ATTN_KV_HEADS * hd
    cos, sin = _rope_tables(lay)
    b, n = lay.batch, lay.n
    tab_spec = pl.BlockSpec((TOKEN_TILE, hd), lambda bb, t: (t, 0))
    return pl.pallas_call(
        _gqa_in_kernel,
        grid=(b, lay.tiles),
        in_specs=[
            _tok_spec(d, lay), _full_spec((1, d)), _mod_spec(lay), _full_spec((d, nq + 2 * nk)),
            _full_spec((1, hd)), _full_spec((1, hd)), tab_spec, tab_spec,
        ],
        out_specs=[_tok_spec(nq, lay), _tok_spec(nk, lay), _tok_spec(nk, lay)],
        out_shape=[
            jax.ShapeDtypeStruct((b, n, nq), BF16),
            jax.ShapeDtypeStruct((b, n, nk), BF16),
            jax.ShapeDtypeStruct((b, n, nk), BF16),
        ],
        compiler_params=_cparams(("parallel", "parallel")),
        name="gqa_in",
    )(h, norm_g.reshape(1, d), mod_l, w_in.astype(BF16), q_g.reshape(1, hd), k_g.reshape(1, hd), cos, sin)


def _attn_kernel(q_ref, k_ref, v_ref, o_ref, *, ctx_tiles, n_ctx, n_all):
    hd = ATTN_HEAD_DIM
    t = pl.program_id(2)
    q = q_ref[0]
    rows = q.shape[0]
    q2 = jnp.concatenate([q[:, g * hd:(g + 1) * hd] for g in range(ATTN_GROUP)], axis=0)

    def attend(n_keys):
        s = _dot_nt(q2, k_ref[0, :n_keys, :])
        p = jnp.exp(s - jnp.max(s, axis=-1, keepdims=True))
        o = _dot(p.astype(BF16), v_ref[0, :n_keys, :]) / jnp.sum(p, axis=-1, keepdims=True)
        for g in range(ATTN_GROUP):
            o_ref[0, :, g * hd:(g + 1) * hd] = o[g * rows:(g + 1) * rows].astype(BF16)

    @pl.when(t < ctx_tiles)
    def _():
        attend(n_ctx)

    @pl.when(t >= ctx_tiles)
    def _():
        attend(n_all)


def _attention(q, k, v, lay):
    hd = ATTN_HEAD_DIM
    b, n = lay.batch, lay.n
    gw = ATTN_GROUP * hd
    kern = functools.partial(_attn_kernel, ctx_tiles=lay.ctx_tiles, n_ctx=lay.n_ctx, n_all=n)
    return pl.pallas_call(
        kern,
        grid=(b, ATTN_KV_HEADS, lay.tiles),
        in_specs=[
            pl.BlockSpec((1, TOKEN_TILE, gw), lambda bb, kh, t: (bb, t, kh)),
            pl.BlockSpec((1, n, hd), lambda bb, kh, t: (bb, 0, kh)),
            pl.BlockSpec((1, n, hd), lambda bb, kh, t: (bb, 0, kh)),
        ],
        out_specs=pl.BlockSpec((1, TOKEN_TILE, gw), lambda bb, kh, t: (bb, t, kh)),
        out_shape=jax.ShapeDtypeStruct((b, n, ATTN_HEADS * hd), BF16),
        compiler_params=_cparams(("parallel", "parallel", "parallel")),
        name="attention",
    )(q, k, v)


def _gla_in_kernel(h_ref, g_ref, mod_ref, w_ref, wa_ref, aw_ref, ab_ref,
                   q_ref, k_ref, v_ref, gate_ref, laf_ref, lab_ref):
    d = D_MODEL
    mod = mod_ref[0]
    u = _norm_mod(h_ref[0], g_ref[...], mod[0:1], mod[1:2]).astype(BF16)
    p = _dot(u, w_ref[...])
    q_ref[0] = p[:, :GLA_QK] * GLA_DK ** -0.5
    k_ref[0] = p[:, GLA_QK:2 * GLA_QK]
    v_ref[0] = p[:, 2 * GLA_QK:2 * GLA_QK + d].astype(BF16)
    gate_ref[0] = p[:, 2 * GLA_QK + d:]
    a_low = _dot(u, wa_ref[...]).astype(BF16)
    pre = _dot(a_low, aw_ref[...]) + ab_ref[...]
    la = _log_sigmoid(pre) * (1.0 / GLA_TAU)
    laf_ref[0] = la[:, :GLA_QK]
    lab_ref[0] = la[:, GLA_QK:]


def _gla_in(h, norm_g, mod_l, w_in, alpha_w2, alpha_b, lay):
    d, r = D_MODEL, GLA_GATE_RANK
    n_main = 2 * GLA_QK + 2 * d
    w_main = w_in[:, :n_main].astype(BF16)
    wa = jnp.zeros((d, LANES), F32).at[:, :2 * r].set(w_in[:, n_main:]).astype(BF16)
    aw = jnp.zeros((LANES, 2 * GLA_QK), F32)
    aw = aw.at[:r, :GLA_QK].set(alpha_w2[0]).at[r:2 * r, GLA_QK:].set(alpha_w2[1]).astype(BF16)
    ab = alpha_b.reshape(1, 2 * GLA_QK)
    b, n = lay.batch, lay.n
    return pl.pallas_call(
        _gla_in_kernel,
        grid=(b, lay.tiles),
        in_specs=[
            _tok_spec(d, lay), _full_spec((1, d)), _mod_spec(lay), _full_spec((d, n_main)),
            _full_spec((d, LANES)), _full_spec((LANES, 2 * GLA_QK)), _full_spec((1, 2 * GLA_QK)),
        ],
        out_specs=[_tok_spec(GLA_QK, lay), _tok_spec(GLA_QK, lay), _tok_spec(d, lay), _tok_spec(d, lay),
                   _tok_spec(GLA_QK, lay), _tok_spec(GLA_QK, lay)],
        out_shape=[
            jax.ShapeDtypeStruct((b, n, GLA_QK), F32),
            jax.ShapeDtypeStruct((b, n, GLA_QK), F32),
            jax.ShapeDtypeStruct((b, n, d), BF16),
            jax.ShapeDtypeStruct((b, n, d), F32),
            jax.ShapeDtypeStruct((b, n, GLA_QK), F32),
            jax.ShapeDtypeStruct((b, n, GLA_QK), F32),
        ],
        compiler_params=_cparams(("parallel", "parallel")),
        name="gla_in",
    )(h, norm_g.reshape(1, d), mod_l, w_main, wa, aw, ab)


def _gla_scan_kernel(q_ref, k_ref, v_ref, la_ref, out_ref, st_ref, *, reverse):
    T, C = GLA_TILE, GLA_CHUNK
    dk, dv, nh = GLA_DK, GLA_DV, GLA_HEADS
    j = pl.program_id(1)

    @pl.when(j == 0)
    def _():
        st_ref[...] = jnp.zeros_like(st_ref)

    row = lax.broadcasted_iota(I32, (T, T), 0)
    col = lax.broadcasted_iota(I32, (T, T), 1)
    same_chunk = (row // C) == (col // C)
    order = (col >= row) if reverse else (col <= row)
    cum_mask = (same_chunk & order).astype(F32)
    b_all = jnp.dot(cum_mask, la_ref[0], precision=HIGHEST, preferred_element_type=F32)
    crow = lax.broadcasted_iota(I32, (C, C), 0)
    ccol = lax.broadcasted_iota(I32, (C, C), 1)
    causal = (ccol >= crow) if reverse else (ccol <= crow)

    chunks = range(T // C)
    for c in (reversed(chunks) if reverse else chunks):
        lo = c * C
        b = b_all[lo:lo + C]
        b_end = b[0:1] if reverse else b[C - 1:C]
        b_mid = b[C // 2:C // 2 + 1]
        qc = q_ref[0, lo:lo + C, :]
        kc = k_ref[0, lo:lo + C, :]
        q_state = (qc * jnp.exp(b)).astype(BF16)
        k_state = (kc * jnp.exp(b_end - b)).astype(BF16)
        q_mid = (qc * jnp.exp(b - b_mid)).astype(BF16)
        k_mid = (kc * jnp.exp(b_mid - b)).astype(BF16)
        decay = jnp.exp(b_end)
        for h in range(nh):
            ks = slice(h * dk, (h + 1) * dk)
            vs = slice(h * dv, (h + 1) * dv)
            v_h = v_ref[0, lo:lo + C, vs]
            a = jnp.where(causal, _dot_nt(q_mid[:, ks], k_mid[:, ks]), 0.0)
            st = st_ref[h]
            out_ref[0, lo:lo + C, vs] = _dot_nt(q_state[:, ks], st.astype(BF16)) + _dot(a.astype(BF16), v_h)
            st_ref[h] = st * decay[:, ks] + _dot_tn(v_h, k_state[:, ks])


def _gla_scan(q, k, v, la, lay, reverse):
    b, n = lay.batch, lay.n
    n_tiles, ctx_tiles = n // GLA_TILE, lay.n_ctx // GLA_TILE

    def tidx(j):
        return _scan_chunk_index(reverse, j, n_tiles, ctx_tiles)

    kern = functools.partial(_gla_scan_kernel, reverse=reverse)
    return pl.pallas_call(
        kern,
        grid=(b, n_tiles),
        in_specs=[
            pl.BlockSpec((1, GLA_TILE, GLA_QK), lambda bb, j: (bb, tidx(j), 0)),
            pl.BlockSpec((1, GLA_TILE, GLA_QK), lambda bb, j: (bb, tidx(j), 0)),
            pl.BlockSpec((1, GLA_TILE, D_MODEL), lambda bb, j: (bb, tidx(j), 0)),
            pl.BlockSpec((1, GLA_TILE, GLA_QK), lambda bb, j: (bb, tidx(j), 0)),
        ],
        out_specs=pl.BlockSpec((1, GLA_TILE, D_MODEL), lambda bb, j: (bb, tidx(j), 0)),
        out_shape=jax.ShapeDtypeStruct((b, n, D_MODEL), F32),
        scratch_shapes=[pltpu.VMEM((GLA_HEADS, GLA_DV, GLA_DK), F32)],
        compiler_params=_cparams(("parallel", "arbitrary")),
        name="gla_scan_bwd" if reverse else "gla_scan_fwd",
    )(q, k, v, la)


def _readout_kernel(*refs, mode):
    if mode == "attn":
        a_ref, wout_ref = refs[:2]
        rest = refs[2:]
        y_in = a_ref[0]
    else:
        a_ref, b_ref, gsrc_ref, ng_ref, wout_ref = refs[:5]
        rest = refs[5:]
        hs = a_ref[0] + b_ref[0]
        if mode == "mlstm":
            y_in = _head_rms(hs, MLSTM_DV) * ng_ref[...] * _sigmoid(gsrc_ref[0])
        else:
            gsrc = gsrc_ref[0]
            y_in = _head_rms(hs, GLA_DV) * ng_ref[...] * (gsrc * _sigmoid(gsrc))
        y_in = y_in.astype(BF16)
    (h_ref, mod_ref, n2g_ref, wrh_ref, wrl_ref, br_ref,
     hnew_ref, v_ref, e_ref, gate_ref, rank_ref, cnt_ref, carry_ref) = rest

    first = (pl.program_id(0) == 0) & (pl.program_id(1) == 0)

    @pl.when(first)
    def _():
        carry_ref[...] = jnp.zeros_like(carry_ref)

    mod = mod_ref[0]
    hn = h_ref[0] + mod[2:3] * _dot(y_in, wout_ref[...])
    hnew_ref[0] = hn
    v = _norm_mod(hn, n2g_ref[...], mod[3:4], mod[4:5])
    v_ref[0] = v

    v_hi = v.astype(BF16)
    v_lo = (v - v_hi.astype(F32)).astype(BF16)
    w_hi = wrh_ref[...]
    logits = _dot(v_hi, w_hi) + _dot(v_lo, w_hi) + _dot(v_hi, wrl_ref[...]) + br_ref[...]

    tm = logits.shape[0]
    lane = lax.broadcasted_iota(I32, logits.shape, 1)
    lane_f = lane.astype(F32)
    work = logits
    top_e = jnp.zeros(logits.shape, F32)
    top_p = jnp.zeros(logits.shape, F32)
    onehot = jnp.zeros(logits.shape, F32)
    m0 = None
    for jx in range(TOP_K):
        mx = jnp.max(work, axis=-1, keepdims=True)
        idx = jnp.min(jnp.where(work == mx, lane_f, float(LANES)), axis=-1, keepdims=True)
        if jx == 0:
            m0 = mx
        hit = lane_f == idx
        top_e = jnp.where(lane == jx, idx, top_e)
        top_p = jnp.where(lane == jx, jnp.exp(mx - m0), top_p)
        onehot = jnp.where(hit, 1.0, onehot)
        work = jnp.where(hit, -jnp.inf, work)
    e_ref[0] = top_e.astype(I32)
    gate_ref[0] = top_p / jnp.sum(top_p, axis=-1, keepdims=True)

    r = lax.broadcasted_iota(I32, (tm, tm), 0)
    c = lax.broadcasted_iota(I32, (tm, tm), 1)
    tril = (c <= r).astype(BF16)
    cum = _dot(tril, onehot.astype(BF16))
    carry = carry_ref[0:1, :]
    rank_ref[0] = (carry + cum - onehot).astype(I32)
    total = carry + cum[tm - 1:tm, :]
    carry_ref[...] = jnp.broadcast_to(total, carry_ref.shape)
    cnt_ref[...] = jnp.broadcast_to(total, cnt_ref.shape)


def _readout(mode, mixer_outs, w_out, h, mod_l, norm2_g, router_w, router_b, lay):
    d = D_MODEL
    b, n = lay.batch, lay.n
    wr = jnp.zeros((d, LANES), F32).at[:, :N_EXPERTS].set(router_w)
    wr_hi = wr.astype(BF16)
    wr_lo = (wr - wr_hi.astype(F32)).astype(BF16)
    br = jnp.full((1, LANES), NEG_BIG, F32).at[0, :N_EXPERTS].set(router_b)
    if mode == "attn":
        (attn_o,) = mixer_outs
        head_in = [attn_o]
        head_specs = [_tok_spec(d, lay)]
    elif mode == "mlstm":
        h_dir, o_gate, norm_g = mixer_outs
        head_in = [h_dir, h_dir, o_gate, norm_g.reshape(1, d)]
        head_specs = [pl.BlockSpec((None, 1, TOKEN_TILE, d), lambda bb, t: (0, bb, t, 0)),
                      pl.BlockSpec((None, 1, TOKEN_TILE, d), lambda bb, t: (1, bb, t, 0)),
                      _tok_spec(d, lay), _full_spec((1, d))]
    else:
        o_f, o_b, g_gate, norm_g = mixer_outs
        head_in = [o_f, o_b, g_gate, norm_g.reshape(1, d)]
        head_specs = [_tok_spec(d, lay), _tok_spec(d, lay), _tok_spec(d, lay), _full_spec((1, d))]
    kern = functools.partial(_readout_kernel, mode=mode)
    return pl.pallas_call(
        kern,
        grid=(b, lay.tiles),
        in_specs=head_specs + [
            _full_spec((d, d)), _tok_spec(d, lay), _mod_spec(lay), _full_spec((1, d)),
            _full_spec((d, LANES)), _full_spec((d, LANES)), _full_spec((1, LANES)),
        ],
        out_specs=[_tok_spec(d, lay), _tok_spec(d, lay), _tok_spec(LANES, lay), _tok_spec(LANES, lay),
                   _tok_spec(LANES, lay), _full_spec((SUBLANES, LANES))],
        out_shape=[
            jax.ShapeDtypeStruct((b, n, d), F32),
            jax.ShapeDtypeStruct((b, n, d), F32),
            jax.ShapeDtypeStruct((b, n, LANES), I32),
            jax.ShapeDtypeStruct((b, n, LANES), F32),
            jax.ShapeDtypeStruct((b, n, LANES), I32),
            jax.ShapeDtypeStruct((SUBLANES, LANES), F32),
        ],
        scratch_shapes=[pltpu.VMEM((SUBLANES, LANES), F32)],
        compiler_params=_cparams(("arbitrary", "arbitrary")),
        name="readout_" + mode,
    )(*head_in, w_out.astype(BF16), h, mod_l, norm2_g.reshape(1, d), wr_hi, wr_lo, br)


def _dest_kernel(e_ref, rank_ref, start_ref, dest_ref):
    pos = (rank_ref[0] + start_ref[...]).astype(F32)
    e = e_ref[0]
    lane = lax.broadcasted_iota(I32, e.shape, 1)
    dest = jnp.zeros(e.shape, F32)
    for jx in range(TOP_K):
        slot = jnp.sum(jnp.where(lane == e[:, jx:jx + 1], pos, 0.0), axis=-1, keepdims=True)
        dest = jnp.where(lane == jx, slot, dest)
    dest_ref[0] = dest.astype(I32)


def _dest(top_e, rank, pad_start, lay):
    b, n = lay.batch, lay.n
    return pl.pallas_call(
        _dest_kernel,
        grid=(b, lay.tiles),
        in_specs=[_tok_spec(LANES, lay), _tok_spec(LANES, lay), _full_spec((1, LANES))],
        out_specs=_tok_spec(LANES, lay),
        out_shape=jax.ShapeDtypeStruct((b, n, LANES), I32),
        compiler_params=_cparams(("parallel", "parallel")),
        name="moe_dest",
    )(top_e, rank, pad_start)


def _row_copies_wait(src_block, dst_rows, sem):
    for _ in range(TOP_K):
        pltpu.make_async_copy(src_block, dst_rows, sem).wait()


def _dispatch_kernel(dest_ref, v_ref, xb_in_ref, xb_ref, sem):
    del xb_in_ref
    rows = v_ref.shape[1]

    def body(r, carry):
        for jx in range(TOP_K):
            slot = dest_ref[0, 0, r * TOP_K + jx]
            pltpu.make_async_copy(v_ref.at[0, pl.ds(r, 1), :], xb_ref.at[pl.ds(slot, 1), :], sem).start()
        return carry

    lax.fori_loop(0, rows, body, 0)
    _row_copies_wait(v_ref.at[0], xb_ref.at[pl.ds(0, rows), :], sem)


def _dispatch(dest_c, v, cap, lay):
    d = D_MODEL
    xb0 = jnp.zeros((cap, d), F32)
    tiles = lay.tiles
    return pl.pallas_call(
        _dispatch_kernel,
        grid=(lay.batch, tiles),
        in_specs=[
            pl.BlockSpec((1, 1, TOKEN_TILE * TOP_K), lambda bb, t: (bb * tiles + t, 0, 0),
                         memory_space=pltpu.SMEM),
            _tok_spec(d, lay),
            pl.BlockSpec(memory_space=pl.ANY),
        ],
        out_specs=pl.BlockSpec(memory_space=pl.ANY),
        out_shape=jax.ShapeDtypeStruct((cap, d), F32),
        scratch_shapes=[pltpu.SemaphoreType.DMA(())],
        input_output_aliases={2: 0},
        compiler_params=_cparams(("arbitrary", "arbitrary")),
        name="moe_dispatch",
    )(dest_c, v, xb0)


def _expert_kernel(be_ref, nu_ref, x_ref, wgu_ref, bgu_ref, wd_ref, bd_ref, y_ref):
    del be_ref
    used = pl.program_id(0) < nu_ref[0]

    @pl.when(jnp.logical_not(used))
    def _():
        y_ref[...] = jnp.zeros_like(y_ref)

    @pl.when(used)
    def _():
        gu = _dot(x_ref[...].astype(BF16), wgu_ref[0]) + bgu_ref[0]
        g = jnp.minimum(gu[:, :D_EXPERT], SWIGLU_LIMIT)
        lin = jnp.clip(gu[:, D_EXPERT:], -SWIGLU_LIMIT, SWIGLU_LIMIT)
        hdn = g * _sigmoid(SWIGLU_ALPHA * g) * (lin + 1.0)
        y_ref[...] = _dot(hdn.astype(BF16), wd_ref[0]) + bd_ref[0]


def _experts(block_e, n_used, xb, w_gu, b_gu, w_d, b_d):
    cap, d = xb.shape
    n_blocks = cap // MOE_BLOCK
    de = D_EXPERT

    def row_map(i, be, nu):
        return (jnp.minimum(i, nu[0] - 1), 0)

    def exp_map(i, be, nu):
        return (be[i], 0, 0)

    grid_spec = pltpu.PrefetchScalarGridSpec(
        num_scalar_prefetch=2,
        grid=(n_blocks,),
        in_specs=[
            pl.BlockSpec((MOE_BLOCK, d), row_map),
            pl.BlockSpec((1, d, 2 * de), exp_map),
            pl.BlockSpec((1, 1, 2 * de), exp_map),
            pl.BlockSpec((1, de, d), exp_map),
            pl.BlockSpec((1, 1, d), exp_map),
        ],
        out_specs=pl.BlockSpec((MOE_BLOCK, d), lambda i, be, nu: (i, 0)),
    )
    return pl.pallas_call(
        _expert_kernel,
        grid_spec=grid_spec,
        out_shape=jax.ShapeDtypeStruct((cap, d), F32),
        compiler_params=_cparams(("arbitrary",)),
        name="moe_experts",
    )(block_e, n_used, xb, w_gu, b_gu.reshape(N_EXPERTS, 1, 2 * de), w_d, b_d.reshape(N_EXPERTS, 1, d))


def _combine_kernel(dest_ref, yb_ref, gate_ref, h_ref, mod_ref, out_ref, ybuf_ref, sem):
    rows = h_ref.shape[1]

    def body(r, carry):
        for jx in range(TOP_K):
            slot = dest_ref[0, 0, r * TOP_K + jx]
            pltpu.make_async_copy(yb_ref.at[pl.ds(slot, 1), :], ybuf_ref.at[jx, pl.ds(r, 1), :], sem).start()
        return carry

    lax.fori_loop(0, rows, body, 0)
    _row_copies_wait(yb_ref.at[pl.ds(0, rows), :], ybuf_ref.at[0], sem)
    gate = gate_ref[0]
    acc = gate[:, 0:1] * ybuf_ref[0]
    for jx in range(1, TOP_K):
        acc = acc + gate[:, jx:jx + 1] * ybuf_ref[jx]
    out_ref[0] = h_ref[0] + mod_ref[0, 5:6, :] * acc


def _combine(dest_c, yb, gate, h, mod_l, lay):
    d = D_MODEL
    tiles = lay.tiles
    return pl.pallas_call(
        _combine_kernel,
        grid=(lay.batch, tiles),
        in_specs=[
            pl.BlockSpec((1, 1, TOKEN_TILE * TOP_K), lambda bb, t: (bb * tiles + t, 0, 0),
                         memory_space=pltpu.SMEM),
            pl.BlockSpec(memory_space=pl.ANY),
            _tok_spec(LANES, lay), _tok_spec(d, lay), _mod_spec(lay),
        ],
        out_specs=_tok_spec(d, lay),
        out_shape=jax.ShapeDtypeStruct((lay.batch, lay.n, d), F32),
        scratch_shapes=[pltpu.VMEM((TOP_K, TOKEN_TILE, d), F32), pltpu.SemaphoreType.DMA(())],
        compiler_params=_cparams(("arbitrary", "arbitrary")),
        name="moe_combine",
    )(dest_c, yb, gate, h, mod_l)


def _moe(h, v, top_e, gate, rank, counts, mod_l, w_gu, b_gu, w_d, b_d, lay):
    n_tokens = lay.batch * lay.n
    n_assign = n_tokens * TOP_K
    cap = (-(-n_assign // MOE_BLOCK) + N_EXPERTS) * MOE_BLOCK
    n_blocks = cap // MOE_BLOCK
    cnt = counts[0, :N_EXPERTS].astype(I32)
    padded = (cnt + MOE_BLOCK - 1) // MOE_BLOCK * MOE_BLOCK
    pad_ends = jnp.cumsum(padded)
    pad_starts = pad_ends - padded
    n_used = (pad_ends[-1] // MOE_BLOCK).astype(I32).reshape(1)
    blk = jnp.minimum(jnp.arange(n_blocks, dtype=I32), n_used[0] - 1) * MOE_BLOCK
    block_e = jnp.minimum(jnp.searchsorted(pad_ends, blk, side="right"), N_EXPERTS - 1).astype(I32)
    start_row = jnp.zeros((1, LANES), I32).at[0, :N_EXPERTS].set(pad_starts)

    dest = _dest(top_e, rank, start_row, lay)
    dest_c = dest[:, :, :TOP_K].reshape(lay.batch * lay.tiles, 1, TOKEN_TILE * TOP_K)
    xb = _dispatch(dest_c, v, cap, lay)
    yb = _experts(block_e, n_used, xb, w_gu, b_gu, w_d, b_d)
    return _combine(dest_c, yb, gate, h, mod_l, lay)


def kernel(x, c, ctx, c_ctx, norm1_g, norm2_g, mod_w, mod_b,
           mlstm_w_in, mlstm_conv_w, mlstm_gate_b, mlstm_out_norm_g, mlstm_w_out,
           attn_w_in, attn_q_norm_g, attn_k_norm_g, attn_w_out,
           gla_w_in, gla_alpha_w2, gla_alpha_b, gla_out_norm_g, gla_w_out,
           router_w, router_b, moe_w_gu, moe_b_gu, moe_w_down, moe_b_down):
    batch, n_lat, d = x.shape
    n_ctx = ctx.shape[1]
    depth = norm1_g.shape[0]
    assert d == D_MODEL
    lay = _Layout(batch, n_ctx, n_lat)

    cond_rows = -(-(batch + 1) // SUBLANES) * SUBLANES
    cond = jnp.zeros((cond_rows, d), F32).at[:batch].set(c).at[batch].set(c_ctx)
    mod_all = _modulation(cond, mod_w, mod_b)

    h = jnp.concatenate([ctx, x], axis=1)
    for layer in range(depth):
        kind, j = layer % 3, layer // 3
        mod_l = mod_all[layer]
        if kind == 0:
            qk_pre, v_m, o_gate, gcol, grow = _mlstm_in(h, norm1_g[layer], mod_l, mlstm_w_in[j],
                                                        mlstm_gate_b[j], lay)
            h_dir = _mlstm_scan(qk_pre, v_m, gcol, grow, mlstm_conv_w[j], lay)
            mixer = ("mlstm", (h_dir, o_gate, mlstm_out_norm_g[j]), mlstm_w_out[j])
        elif kind == 1:
            q_a, k_a, v_a = _gqa_in(h, norm1_g[layer], mod_l, attn_w_in[j], attn_q_norm_g[j],
                                    attn_k_norm_g[j], lay)
            mixer = ("attn", (_attention(q_a, k_a, v_a, lay),), attn_w_out[j])
        else:
            q_g, k_g, v_g, g_gate, la_f, la_b = _gla_in(h, norm1_g[layer], mod_l, gla_w_in[j],
                                                        gla_alpha_w2[j], gla_alpha_b[j], lay)
            o_f = _gla_scan(q_g, k_g, v_g, la_f, lay, reverse=False)
            o_b = _gla_scan(q_g, k_g, v_g, la_b, lay, reverse=True)
            mixer = ("gla", (o_f, o_b, g_gate, gla_out_norm_g[j]), gla_w_out[j])
        mode, mixer_outs, w_out = mixer
        h, v_moe, top_e, gate, rank, counts = _readout(mode, mixer_outs, w_out, h, mod_l, norm2_g[layer],
                                                       router_w[layer], router_b[layer], lay)
        h = _moe(h, v_moe, top_e, gate, rank, counts, mod_l,
                 moe_w_gu[layer].astype(BF16), moe_b_gu[layer], moe_w_down[layer].astype(BF16),
                 moe_b_down[layer], lay)
    return h[:, n_ctx:]
```

```python
import functools

import jax
import jax.numpy as jnp
from jax import lax
from jax.experimental import pallas as pl
from jax.experimental.pallas import tpu as pltpu

F32 = jnp.float32
BF16 = jnp.bfloat16
I32 = jnp.int32
HIGHEST = lax.Precision.HIGHEST

D_MODEL = 1024
GRID_W = 64
NORM_EPS = 1e-6

MLSTM_HEADS = 8
MLSTM_DQK = 64
MLSTM_DV = 128
MLSTM_QK = MLSTM_HEADS * MLSTM_DQK

ATTN_HEADS = 8
ATTN_KV_HEADS = 4
ATTN_GROUP = ATTN_HEADS // ATTN_KV_HEADS
ATTN_HEAD_DIM = 128
ROPE_THETA = 10000.0

GLA_HEADS = 4
GLA_DK = 128
GLA_DV = 256
GLA_GATE_RANK = 16
GLA_TAU = 16.0
GLA_QK = GLA_HEADS * GLA_DK

N_EXPERTS = 32
TOP_K = 4
D_EXPERT = D_MODEL
SWIGLU_LIMIT = 7.0
SWIGLU_ALPHA = 1.702

LANES = 128
SUBLANES = 8
TOKEN_TILE = 256
MLSTM_CHUNK = 128
GLA_TILE = 128
GLA_CHUNK = 32
ATTN_KV_CHUNK = 512
ATTN_Q_SCALE = ATTN_HEAD_DIM ** -0.5 * 1.4426950408889634
MOE_BLOCK = 256
ROW_TILE = D_MODEL // LANES
assert ROW_TILE == SUBLANES
DMA_UNROLL = 8
VMEM_LIMIT = 48 * 1024 * 1024
EXPERT_VMEM_LIMIT = 56 * 1024 * 1024
NEG_BIG = -1e30


def _cparams(semantics, vmem=VMEM_LIMIT):
    return pltpu.CompilerParams(dimension_semantics=semantics, vmem_limit_bytes=vmem)


def _log_sigmoid(x):
    return jnp.minimum(x, 0.0) - jnp.log(1.0 + jnp.exp(-jnp.abs(x)))


def _sigmoid(x):
    return 1.0 / (1.0 + jnp.exp(-x))


def _norm_mod(x, g, shift, scale):
    ms = jnp.mean(x * x, axis=-1, keepdims=True)
    y = x * lax.rsqrt(ms + NORM_EPS) * g
    return y * (1.0 + scale) + shift


def _head_rms(x, head_dim):
    outs = []
    for i in range(x.shape[1] // head_dim):
        xs = x[:, i * head_dim:(i + 1) * head_dim]
        ms = jnp.mean(xs * xs, axis=-1, keepdims=True)
        outs.append(xs * lax.rsqrt(ms + NORM_EPS))
    return jnp.concatenate(outs, axis=1)


def _dot(a, b):
    return jnp.dot(a, b, preferred_element_type=F32)


def _dot_nt(a, b):
    return lax.dot_general(a, b, (((1,), (1,)), ((), ())), preferred_element_type=F32)


def _dot_tn(a, b):
    return lax.dot_general(a, b, (((0,), (0,)), ((), ())), preferred_element_type=F32)


def _mod_kernel(c_ref, w_ref, b_ref, o_ref):
    x = c_ref[...]
    x = x * _sigmoid(x)
    o_ref[0, 0] = _dot(x.astype(BF16), w_ref[0].astype(BF16)) + b_ref[0, 0]


def _modulation(cond_rows, mod_w, mod_b):
    depth, d, _ = mod_w.shape
    rows = cond_rows.shape[0]
    out = pl.pallas_call(
        _mod_kernel,
        grid=(depth, 6),
        in_specs=[
            pl.BlockSpec((rows, d), lambda l, j: (0, 0)),
            pl.BlockSpec((1, d, d), lambda l, j: (l, 0, j)),
            pl.BlockSpec((1, 1, 1, d), lambda l, j: (l, j, 0, 0)),
        ],
        out_specs=pl.BlockSpec((1, 1, rows, d), lambda l, j: (l, j, 0, 0)),
        out_shape=jax.ShapeDtypeStruct((depth, 6, rows, d), F32),
        compiler_params=_cparams(("parallel", "parallel")),
        name="modulation",
    )(cond_rows, mod_w, mod_b.reshape(depth, 6, 1, d))
    return out.transpose(0, 2, 1, 3)


class _Layout:
    def __init__(self, batch, n_ctx, n_lat):
        self.batch = batch
        self.n_ctx = n_ctx
        self.n_lat = n_lat
        self.n = n_ctx + n_lat
        assert n_ctx % TOKEN_TILE == 0 and n_lat % TOKEN_TILE == 0
        assert n_ctx % MLSTM_CHUNK == 0 and n_lat % MLSTM_CHUNK == 0
        assert n_ctx % GLA_TILE == 0 and n_lat % GLA_TILE == 0
        self.tiles = self.n // TOKEN_TILE
        self.ctx_tiles = n_ctx // TOKEN_TILE
        self.ctx_row = batch

    def mod_index(self, b, t):
        return jnp.where(t < self.ctx_tiles, self.ctx_row, b)


def _tok_spec(width, lay):
    return pl.BlockSpec((1, TOKEN_TILE, width), lambda b, t: (b, t, 0))


def _full_spec(shape):
    nd = len(shape)
    return pl.BlockSpec(shape, lambda b, t: (0,) * nd)


def _mod_spec(lay):
    return pl.BlockSpec((1, 6, D_MODEL), lambda b, t: (lay.mod_index(b, t), 0, 0))


def _mlstm_in_kernel(h_ref, g_ref, mod_ref, w_ref, wg_ref, wgt_ref, gb_ref, gbt_ref,
                     qk_ref, v_ref, o_ref, gc_ref, gr_ref):
    mod = mod_ref[0]
    u = _norm_mod(h_ref[0], g_ref[...], mod[0:1], mod[1:2]).astype(BF16)
    p = _dot(u, w_ref[...])
    qk_ref[0] = p[:, :2 * MLSTM_QK]
    v_ref[0] = p[:, 2 * MLSTM_QK:2 * MLSTM_QK + D_MODEL].astype(BF16)
    o_ref[0] = p[:, 2 * MLSTM_QK + D_MODEL:]
    gc = _dot(u, wg_ref[...]) + gb_ref[...]
    lane = lax.broadcasted_iota(I32, gc.shape, 1)
    gc_ref[0] = jnp.where(((lane >> 3) & 1) == 1, _log_sigmoid(gc), gc)
    gr = _dot_nt(wgt_ref[...], u) + gbt_ref[...]
    sub = lax.broadcasted_iota(I32, gr.shape, 0)
    gr_ref[0] = jnp.where(((sub >> 3) & 1) == 1, _log_sigmoid(gr), gr)


def _mlstm_in(h, norm_g, mod_l, w_in, gate_b, lay):
    d = D_MODEL
    n_main = 2 * MLSTM_QK + 2 * d
    n_gate = 4 * MLSTM_HEADS
    w_main = w_in[:, :n_main].astype(BF16)
    w_gate = w_in[:, n_main:]
    wg = jnp.zeros((d, LANES), F32).at[:, :n_gate].set(w_gate).astype(BF16)
    wgt = w_gate.T.astype(BF16)
    gb = jnp.zeros((1, LANES), F32).at[0, :n_gate].set(gate_b.reshape(-1))
    gbt = gate_b.reshape(n_gate, 1)
    b, n = lay.batch, lay.n
    return pl.pallas_call(
        _mlstm_in_kernel,
        grid=(b, lay.tiles),
        in_specs=[
            _tok_spec(d, lay), _full_spec((1, d)), _mod_spec(lay),
            _full_spec((d, n_main)), _full_spec((d, LANES)), _full_spec((n_gate, d)),
            _full_spec((1, LANES)), _full_spec((n_gate, 1)),
        ],
        out_specs=[
            _tok_spec(2 * MLSTM_QK, lay), _tok_spec(d, lay), _tok_spec(d, lay), _tok_spec(LANES, lay),
            pl.BlockSpec((1, n_gate, TOKEN_TILE), lambda bb, t: (bb, 0, t)),
        ],
        out_shape=[
            jax.ShapeDtypeStruct((b, n, 2 * MLSTM_QK), F32),
            jax.ShapeDtypeStruct((b, n, d), BF16),
            jax.ShapeDtypeStruct((b, n, d), F32),
            jax.ShapeDtypeStruct((b, n, LANES), F32),
            jax.ShapeDtypeStruct((b, n_gate, n), F32),
        ],
        compiler_params=_cparams(("parallel", "parallel")),
        name="mlstm_in",
    )(h, norm_g.reshape(1, d), mod_l, w_main, wg, wgt, gb, gbt)


def _scan_chunk_index(reverse, j, n_chunks, ctx_chunks):
    back = jnp.where(j < ctx_chunks, ctx_chunks - 1 - j, n_chunks - 1 - (j - ctx_chunks))
    return jnp.where(reverse, back, j)


def _mlstm_scan_kernel(qk_ref, qkp_ref, qkn_ref, v_ref, gc_ref, gr_ref, cw_ref, out_ref, st_ref, m_ref,
                       *, n_chunks, ctx_chunks):
    L = MLSTM_CHUNK
    dk, dv, nh = MLSTM_DQK, MLSTM_DV, MLSTM_HEADS
    d = pl.program_id(1)
    j = pl.program_id(2)
    c = _scan_chunk_index(d == 1, j, n_chunks, ctx_chunks)

    @pl.when(j == 0)
    def _():
        st_ref[...] = jnp.zeros_like(st_ref)
        m_ref[...] = jnp.zeros_like(m_ref)

    x = qk_ref[0]
    seg_first = (c == 0) | (c == ctx_chunks)
    seg_last = (c == ctx_chunks - 1) | (c == n_chunks - 1)
    prev_row = jnp.where(seg_first, 0.0, qkp_ref[0, SUBLANES - 1:SUBLANES, :])
    next_row = jnp.where(seg_last, 0.0, qkn_ref[0, 0:1, :])
    rid = lax.broadcasted_iota(I32, x.shape, 0)
    x_m1 = jnp.where(rid == 0, prev_row, pltpu.roll(x, 1, 0))
    x_p1 = jnp.where(rid == L - 1, next_row, pltpu.roll(x, L - 1, 0))
    cw = cw_ref[...]
    y = cw[0:1] * x_m1 + cw[1:2] * x + cw[2:3] * x_p1
    qk = y * _sigmoid(y)
    q = (qk[:, :MLSTM_QK] * dk ** -0.5).astype(BF16)
    k_t = qk[:, MLSTM_QK:].T.astype(BF16)
    v = v_ref[0]

    gcol = gc_ref[0]
    gcol = jnp.where(d == 0, gcol, pltpu.roll(gcol, LANES - 2 * nh, 1))
    grow = gr_ref[0, pl.ds(pl.multiple_of(d * 2 * nh, 2 * nh), 2 * nh), :]

    row = lax.broadcasted_iota(I32, (L, L), 0)
    col = lax.broadcasted_iota(I32, (L, L), 1)
    sgn = 1 - 2 * d
    mask = sgn * (row - col) >= 0
    mask_f = mask.astype(F32)
    mask_tf = (sgn * (col - row) >= 0).astype(F32)
    b_col = jnp.dot(mask_f, gcol, precision=HIGHEST, preferred_element_type=F32)
    b_row = jnp.dot(grow, mask_tf, precision=HIGHEST, preferred_element_type=F32)
    b_tot = jnp.sum(gcol, axis=0, keepdims=True)

    ones_col = (lax.broadcasted_iota(I32, (L, LANES), 1) == 0).astype(BF16)
    for h in range(nh):
        bc = b_col[:, nh + h:nh + h + 1]
        br = b_row[nh + h:nh + h + 1, :]
        li_r = grow[h:h + 1, :]
        li_c = gcol[:, h:h + 1]
        m_prev = m_ref[h, 0:1, 0:1]
        dmat = jnp.where(mask, bc - br + li_r, -jnp.inf)
        inter = bc + m_prev
        m_t = jnp.maximum(inter, jnp.max(dmat, axis=-1, keepdims=True))
        w = jnp.exp(dmat - m_t)
        a_inter = jnp.exp(inter - m_t)
        q_h = q[:, h * dk:(h + 1) * dk]
        kt_h = k_t[h * dk:(h + 1) * dk, :]
        s = _dot(q_h, kt_h) * w
        v_aug = jnp.concatenate([v[:, h * dv:(h + 1) * dv], ones_col], axis=1)
        st = st_ref[h]
        r = a_inter * _dot(q_h, st.astype(BF16)) + _dot(s.astype(BF16), v_aug)
        num = r[:, :dv]
        den = r[:, dv:dv + 1]
        out_ref[0, 0, :, h * dv:(h + 1) * dv] = num / jnp.maximum(jnp.abs(den), jnp.exp(-m_t))
        b_last = b_tot[:, nh + h:nh + h + 1]
        g = b_last - bc + li_c
        m_new = jnp.maximum(b_last + m_prev, jnp.max(g, axis=0, keepdims=True))
        decay = jnp.exp(b_last + m_prev - m_new)
        wg = jnp.exp(g - m_new)
        st_ref[h] = decay * st + _dot(kt_h, (wg * v_aug.astype(F32)).astype(BF16))
        m_ref[h] = jnp.broadcast_to(m_new, (SUBLANES, LANES))


def _mlstm_scan(qk_pre, v, gcol, grow, conv_w, lay):
    b, n = lay.batch, lay.n
    L = MLSTM_CHUNK
    n_chunks, ctx_chunks = n // L, lay.n_ctx // L
    halo = L // SUBLANES
    n_halo = n // SUBLANES

    def cidx(d, j):
        return _scan_chunk_index(d == 1, j, n_chunks, ctx_chunks)

    kern = functools.partial(_mlstm_scan_kernel, n_chunks=n_chunks, ctx_chunks=ctx_chunks)
    return pl.pallas_call(
        kern,
        grid=(b, 2, n_chunks),
        in_specs=[
            pl.BlockSpec((1, L, 2 * MLSTM_QK), lambda bb, d, j: (bb, cidx(d, j), 0)),
            pl.BlockSpec((1, SUBLANES, 2 * MLSTM_QK),
                         lambda bb, d, j: (bb, jnp.maximum(cidx(d, j) * halo - 1, 0), 0)),
            pl.BlockSpec((1, SUBLANES, 2 * MLSTM_QK),
                         lambda bb, d, j: (bb, jnp.minimum((cidx(d, j) + 1) * halo, n_halo - 1), 0)),
            pl.BlockSpec((1, L, D_MODEL), lambda bb, d, j: (bb, cidx(d, j), 0)),
            pl.BlockSpec((1, L, LANES), lambda bb, d, j: (bb, cidx(d, j), 0)),
            pl.BlockSpec((1, 4 * MLSTM_HEADS, L), lambda bb, d, j: (bb, 0, cidx(d, j))),
            pl.BlockSpec((3, 2 * MLSTM_QK), lambda bb, d, j: (0, 0)),
        ],
        out_specs=pl.BlockSpec((1, 1, L, D_MODEL), lambda bb, d, j: (d, bb, cidx(d, j), 0)),
        out_shape=jax.ShapeDtypeStruct((2, b, n, D_MODEL), F32),
        scratch_shapes=[
            pltpu.VMEM((MLSTM_HEADS, MLSTM_DQK, MLSTM_DV + LANES), F32),
            pltpu.VMEM((MLSTM_HEADS, SUBLANES, LANES), F32),
        ],
        compiler_params=_cparams(("parallel", "parallel", "arbitrary")),
        name="mlstm_scan",
    )(qk_pre, qk_pre, qk_pre, v, gcol, grow, conv_w)


def _gqa_in_kernel(h_ref, g_ref, mod_ref, w_ref, qg_ref, kg_ref, cos_ref, sin_ref, q_ref, k_ref, v_ref):
    hd = ATTN_HEAD_DIM
    mod = mod_ref[0]
    u = _norm_mod(h_ref[0], g_ref[...], mod[0:1], mod[1:2]).astype(BF16)
    p = _dot(u, w_ref[...])
    cos = cos_ref[...]
    sin = sin_ref[...]
    lane = lax.broadcasted_iota(I32, cos.shape, 1)
    first_half = (lane & (hd // 4)) == 0

    def rope(xh):
        swapped = jnp.where(first_half, pltpu.roll(xh, hd - hd // 4, 1), pltpu.roll(xh, hd // 4, 1))
        return xh * cos + swapped * sin

    nq, nk = ATTN_HEADS * hd, ATTN_KV_HEADS * hd
    qn = _head_rms(p[:, :nq], hd)
    kn = _head_rms(p[:, nq:nq + nk], hd)
    qg = qg_ref[...]
    kg = kg_ref[...]
    for i in range(ATTN_HEADS):
        q_ref[0, :, i * hd:(i + 1) * hd] = (rope(qn[:, i * hd:(i + 1) * hd] * qg) * ATTN_Q_SCALE).astype(BF16)
    for i in range(ATTN_KV_HEADS):
        k_ref[0, :, i * hd:(i + 1) * hd] = rope(kn[:, i * hd:(i + 1) * hd] * kg).astype(BF16)
    v_ref[0] = p[:, nq + nk:].astype(BF16)


def _rope_tables(lay):
    hd = ATTN_HEAD_DIM
    quarter = hd // 4
    inv = ROPE_THETA ** (-jnp.arange(quarter, dtype=F32) / quarter)
    pos = jnp.arange(lay.n_lat)
    rows = (pos // GRID_W).astype(F32)
    cols = (pos % GRID_W).astype(F32)
    ang = jnp.concatenate([jnp.tile(rows[:, None] * inv, (1, 2)), jnp.tile(cols[:, None] * inv, (1, 2))], axis=1)
    sign = jnp.tile(jnp.concatenate([-jnp.ones(quarter, F32), jnp.ones(quarter, F32)]), 2)
    cos = jnp.concatenate([jnp.ones((lay.n_ctx, hd), F32), jnp.cos(ang)], axis=0)
    sin = jnp.concatenate([jnp.zeros((lay.n_ctx, hd), F32), jnp.sin(ang) * sign], axis=0)
    return cos, sin


def _gqa_in(h, norm_g, mod_l, w_in, q_g, k_g, lay):
    d, hd = D_MODEL, ATTN_HEAD_DIM
    nq, nk = ATTN_HEADS * hd, ATTN_KV_HEADS * hd
    cos, sin = _rope_tables(lay)
    b, n = lay.batch, lay.n
    tab_spec = pl.BlockSpec((TOKEN_TILE, hd), lambda bb, t: (t, 0))
    return pl.pallas_call(
        _gqa_in_kernel,
        grid=(b, lay.tiles),
        in_specs=[
            _tok_spec(d, lay), _full_spec((1, d)), _mod_spec(lay), _full_spec((d, nq + 2 * nk)),
            _full_spec((1, hd)), _full_spec((1, hd)), tab_spec, tab_spec,
        ],
        out_specs=[_tok_spec(nq, lay), _tok_spec(nk, lay), _tok_spec(nk, lay)],
        out_shape=[
            jax.ShapeDtypeStruct((b, n, nq), BF16),
            jax.ShapeDtypeStruct((b, n, nk), BF16),
            jax.ShapeDtypeStruct((b, n, nk), BF16),
        ],
        compiler_params=_cparams(("parallel", "parallel")),
        name="gqa_in",
    )(h, norm_g.reshape(1, d), mod_l, w_in.astype(BF16), q_g.reshape(1, hd), k_g.reshape(1, hd), cos, sin)


def _attn_kernel(q_ref, k_ref, v_ref, o_ref, *, ctx_tiles, n_ctx, n_all):
    hd = ATTN_HEAD_DIM
    t = pl.program_id(2)
    q = q_ref[0]
    rows = q.shape[0]
    q2 = jnp.concatenate([q[:, g * hd:(g + 1) * hd] for g in range(ATTN_GROUP)], axis=0)

    def attend(n_keys):
        bounds = [0, n_ctx] + list(range(n_ctx + ATTN_KV_CHUNK, n_keys + 1, ATTN_KV_CHUNK))
        assert bounds[-1] == n_keys
        m = l = acc = None
        for lo, hi in zip(bounds[:-1], bounds[1:]):
            s = _dot_nt(q2, k_ref[0, lo:hi, :])
            mc = jnp.max(s, axis=-1, keepdims=True)
            if m is None:
                m_new = mc
                p = jnp.exp2(s - m_new)
                l = jnp.sum(p, axis=-1, keepdims=True)
                acc = _dot(p.astype(BF16), v_ref[0, lo:hi, :])
            else:
                m_new = jnp.maximum(m, mc)
                alpha = jnp.exp2(m - m_new)
                p = jnp.exp2(s - m_new)
                l = alpha * l + jnp.sum(p, axis=-1, keepdims=True)
                acc = alpha * acc + _dot(p.astype(BF16), v_ref[0, lo:hi, :])
            m = m_new
        o = acc / l
        for g in range(ATTN_GROUP):
            o_ref[0, :, g * hd:(g + 1) * hd] = o[g * rows:(g + 1) * rows].astype(BF16)

    @pl.when(t < ctx_tiles)
    def _():
        attend(n_ctx)

    @pl.when(t >= ctx_tiles)
    def _():
        attend(n_all)


def _attention(q, k, v, lay):
    hd = ATTN_HEAD_DIM
    b, n = lay.batch, lay.n
    gw = ATTN_GROUP * hd
    kern = functools.partial(_attn_kernel, ctx_tiles=lay.ctx_tiles, n_ctx=lay.n_ctx, n_all=n)
    return pl.pallas_call(
        kern,
        grid=(b, ATTN_KV_HEADS, lay.tiles),
        in_specs=[
            pl.BlockSpec((1, TOKEN_TILE, gw), lambda bb, kh, t: (bb, t, kh)),
            pl.BlockSpec((1, n, hd), lambda bb, kh, t: (bb, 0, kh)),
            pl.BlockSpec((1, n, hd), lambda bb, kh, t: (bb, 0, kh)),
        ],
        out_specs=pl.BlockSpec((1, TOKEN_TILE, gw), lambda bb, kh, t: (bb, t, kh)),
        out_shape=jax.ShapeDtypeStruct((b, n, ATTN_HEADS * hd), BF16),
        compiler_params=_cparams(("parallel", "parallel", "parallel")),
        name="attention",
    )(q, k, v)


def _gla_in_kernel(h_ref, g_ref, mod_ref, w_ref, wa_ref, aw_ref, ab_ref,
                   q_ref, k_ref, v_ref, gate_ref, laf_ref, lab_ref):
    d = D_MODEL
    mod = mod_ref[0]
    u = _norm_mod(h_ref[0], g_ref[...], mod[0:1], mod[1:2]).astype(BF16)
    p = _dot(u, w_ref[...])
    q_ref[0] = p[:, :GLA_QK] * GLA_DK ** -0.5
    k_ref[0] = p[:, GLA_QK:2 * GLA_QK]
    v_ref[0] = p[:, 2 * GLA_QK:2 * GLA_QK + d].astype(BF16)
    gate_ref[0] = p[:, 2 * GLA_QK + d:]
    a_low = _dot(u, wa_ref[...]).astype(BF16)
    pre = _dot(a_low, aw_ref[...]) + ab_ref[...]
    la = _log_sigmoid(pre) * (1.0 / GLA_TAU)
    laf_ref[0] = la[:, :GLA_QK]
    lab_ref[0] = la[:, GLA_QK:]


def _gla_in(h, norm_g, mod_l, w_in, alpha_w2, alpha_b, lay):
    d, r = D_MODEL, GLA_GATE_RANK
    n_main = 2 * GLA_QK + 2 * d
    w_main = w_in[:, :n_main].astype(BF16)
    wa = jnp.zeros((d, LANES), F32).at[:, :2 * r].set(w_in[:, n_main:]).astype(BF16)
    aw = jnp.zeros((LANES, 2 * GLA_QK), F32)
    aw = aw.at[:r, :GLA_QK].set(alpha_w2[0]).at[r:2 * r, GLA_QK:].set(alpha_w2[1]).astype(BF16)
    ab = alpha_b.reshape(1, 2 * GLA_QK)
    b, n = lay.batch, lay.n
    return pl.pallas_call(
        _gla_in_kernel,
        grid=(b, lay.tiles),
        in_specs=[
            _tok_spec(d, lay), _full_spec((1, d)), _mod_spec(lay), _full_spec((d, n_main)),
            _full_spec((d, LANES)), _full_spec((LANES, 2 * GLA_QK)), _full_spec((1, 2 * GLA_QK)),
        ],
        out_specs=[_tok_spec(GLA_QK, lay), _tok_spec(GLA_QK, lay), _tok_spec(d, lay), _tok_spec(d, lay),
                   _tok_spec(GLA_QK, lay), _tok_spec(GLA_QK, lay)],
        out_shape=[
            jax.ShapeDtypeStruct((b, n, GLA_QK), F32),
            jax.ShapeDtypeStruct((b, n, GLA_QK), F32),
            jax.ShapeDtypeStruct((b, n, d), BF16),
            jax.ShapeDtypeStruct((b, n, d), F32),
            jax.ShapeDtypeStruct((b, n, GLA_QK), F32),
            jax.ShapeDtypeStruct((b, n, GLA_QK), F32),
        ],
        compiler_params=_cparams(("parallel", "parallel")),
        name="gla_in",
    )(h, norm_g.reshape(1, d), mod_l, w_main, wa, aw, ab)


def _gla_scan_kernel(q_ref, k_ref, v_ref, la_ref, out_ref, st_ref, *, reverse):
    T, C = GLA_TILE, GLA_CHUNK
    dk, dv, nh = GLA_DK, GLA_DV, GLA_HEADS
    j = pl.program_id(1)

    @pl.when(j == 0)
    def _():
        st_ref[...] = jnp.zeros_like(st_ref)

    row = lax.broadcasted_iota(I32, (T, T), 0)
    col = lax.broadcasted_iota(I32, (T, T), 1)
    same_chunk = (row // C) == (col // C)
    order = (col >= row) if reverse else (col <= row)
    cum_mask = (same_chunk & order).astype(F32)
    b_all = jnp.dot(cum_mask, la_ref[0], precision=HIGHEST, preferred_element_type=F32)
    crow = lax.broadcasted_iota(I32, (C, C), 0)
    ccol = lax.broadcasted_iota(I32, (C, C), 1)
    causal = (ccol >= crow) if reverse else (ccol <= crow)

    chunks = range(T // C)
    for c in (reversed(chunks) if reverse else chunks):
        lo = c * C
        b = b_all[lo:lo + C]
        b_end = b[0:1] if reverse else b[C - 1:C]
        b_mid = b[C // 2:C // 2 + 1]
        qc = q_ref[0, lo:lo + C, :]
        kc = k_ref[0, lo:lo + C, :]
        q_state = (qc * jnp.exp(b)).astype(BF16)
        k_state = (kc * jnp.exp(b_end - b)).astype(BF16)
        q_mid = (qc * jnp.exp(b - b_mid)).astype(BF16)
        k_mid = (kc * jnp.exp(b_mid - b)).astype(BF16)
        decay = jnp.exp(b_end)
        for h in range(nh):
            ks = slice(h * dk, (h + 1) * dk)
            vs = slice(h * dv, (h + 1) * dv)
            v_h = v_ref[0, lo:lo + C, vs]
            a = jnp.where(causal, _dot_nt(q_mid[:, ks], k_mid[:, ks]), 0.0)
            st = st_ref[h]
            out_ref[0, lo:lo + C, vs] = _dot_nt(q_state[:, ks], st.astype(BF16)) + _dot(a.astype(BF16), v_h)
            st_ref[h] = st * decay[:, ks] + _dot_tn(v_h, k_state[:, ks])


def _gla_scan(q, k, v, la, lay, reverse):
    b, n = lay.batch, lay.n
    n_tiles, ctx_tiles = n // GLA_TILE, lay.n_ctx // GLA_TILE

    def tidx(j):
        return _scan_chunk_index(reverse, j, n_tiles, ctx_tiles)

    kern = functools.partial(_gla_scan_kernel, reverse=reverse)
    return pl.pallas_call(
        kern,
        grid=(b, n_tiles),
        in_specs=[
            pl.BlockSpec((1, GLA_TILE, GLA_QK), lambda bb, j: (bb, tidx(j), 0)),
            pl.BlockSpec((1, GLA_TILE, GLA_QK), lambda bb, j: (bb, tidx(j), 0)),
            pl.BlockSpec((1, GLA_TILE, D_MODEL), lambda bb, j: (bb, tidx(j), 0)),
            pl.BlockSpec((1, GLA_TILE, GLA_QK), lambda bb, j: (bb, tidx(j), 0)),
        ],
        out_specs=pl.BlockSpec((1, GLA_TILE, D_MODEL), lambda bb, j: (bb, tidx(j), 0)),
        out_shape=jax.ShapeDtypeStruct((b, n, D_MODEL), F32),
        scratch_shapes=[pltpu.VMEM((GLA_HEADS, GLA_DV, GLA_DK), F32)],
        compiler_params=_cparams(("parallel", "arbitrary")),
        name="gla_scan_bwd" if reverse else "gla_scan_fwd",
    )(q, k, v, la)


def _readout_kernel(*refs, mode):
    if mode == "attn":
        a_ref, wout_ref = refs[:2]
        rest = refs[2:]
        y_in = a_ref[0]
    else:
        a_ref, b_ref, gsrc_ref, ng_ref, wout_ref = refs[:5]
        rest = refs[5:]
        hs = a_ref[0] + b_ref[0]
        if mode == "mlstm":
            y_in = _head_rms(hs, MLSTM_DV) * ng_ref[...] * _sigmoid(gsrc_ref[0])
        else:
            gsrc = gsrc_ref[0]
            y_in = _head_rms(hs, GLA_DV) * ng_ref[...] * (gsrc * _sigmoid(gsrc))
        y_in = y_in.astype(BF16)
    (h_ref, mod_ref, n2g_ref, wrh_ref, wrl_ref, br_ref,
     hnew_ref, v_ref, e_ref, gate_ref, rank_ref, cnt_ref, carry_ref) = rest

    first = (pl.program_id(0) == 0) & (pl.program_id(1) == 0)

    @pl.when(first)
    def _():
        carry_ref[...] = jnp.zeros_like(carry_ref)

    mod = mod_ref[0]
    hn = h_ref[0] + mod[2:3] * _dot(y_in, wout_ref[...])
    hnew_ref[0] = hn
    v = _norm_mod(hn, n2g_ref[...], mod[3:4], mod[4:5])
    for c in range(ROW_TILE):
        v_ref[0, pl.ds(c, v.shape[0], stride=ROW_TILE), :] = v[:, c * LANES:(c + 1) * LANES]

    v_hi = v.astype(BF16)
    v_lo = (v - v_hi.astype(F32)).astype(BF16)
    w_hi = wrh_ref[...]
    logits = _dot(v_hi, w_hi) + _dot(v_lo, w_hi) + _dot(v_hi, wrl_ref[...]) + br_ref[...]

    tm = logits.shape[0]
    lane = lax.broadcasted_iota(I32, logits.shape, 1)
    lane_f = lane.astype(F32)
    work = logits
    top_e = jnp.zeros(logits.shape, F32)
    top_p = jnp.zeros(logits.shape, F32)
    onehot = jnp.zeros(logits.shape, F32)
    m0 = None
    for jx in range(TOP_K):
        mx = jnp.max(work, axis=-1, keepdims=True)
        idx = jnp.min(jnp.where(work == mx, lane_f, float(LANES)), axis=-1, keepdims=True)
        if jx == 0:
            m0 = mx
        hit = lane_f == idx
        top_e = jnp.where(lane == jx, idx, top_e)
        top_p = jnp.where(lane == jx, jnp.exp(mx - m0), top_p)
        onehot = jnp.where(hit, 1.0, onehot)
        work = jnp.where(hit, -jnp.inf, work)
    e_ref[0] = top_e.astype(I32)
    gate_ref[0] = top_p / jnp.sum(top_p, axis=-1, keepdims=True)

    r = lax.broadcasted_iota(I32, (tm, tm), 0)
    c = lax.broadcasted_iota(I32, (tm, tm), 1)
    tril = (c <= r).astype(BF16)
    cum = _dot(tril, onehot.astype(BF16))
    carry = carry_ref[0:1, :]
    rank_ref[0] = (carry + cum - onehot).astype(I32)
    total = carry + cum[tm - 1:tm, :]
    carry_ref[...] = jnp.broadcast_to(total, carry_ref.shape)
    cnt_ref[...] = jnp.broadcast_to(total, cnt_ref.shape)


def _readout(mode, mixer_outs, w_out, h, mod_l, norm2_g, router_w, router_b, lay):
    d = D_MODEL
    b, n = lay.batch, lay.n
    wr = jnp.zeros((d, LANES), F32).at[:, :N_EXPERTS].set(router_w)
    wr_hi = wr.astype(BF16)
    wr_lo = (wr - wr_hi.astype(F32)).astype(BF16)
    br = jnp.full((1, LANES), NEG_BIG, F32).at[0, :N_EXPERTS].set(router_b)
    if mode == "attn":
        (attn_o,) = mixer_outs
        head_in = [attn_o]
        head_specs = [_tok_spec(d, lay)]
    elif mode == "mlstm":
        h_dir, o_gate, norm_g = mixer_outs
        head_in = [h_dir, h_dir, o_gate, norm_g.reshape(1, d)]
        head_specs = [pl.BlockSpec((None, 1, TOKEN_TILE, d), lambda bb, t: (0, bb, t, 0)),
                      pl.BlockSpec((None, 1, TOKEN_TILE, d), lambda bb, t: (1, bb, t, 0)),
                      _tok_spec(d, lay), _full_spec((1, d))]
    else:
        o_f, o_b, g_gate, norm_g = mixer_outs
        head_in = [o_f, o_b, g_gate, norm_g.reshape(1, d)]
        head_specs = [_tok_spec(d, lay), _tok_spec(d, lay), _tok_spec(d, lay), _full_spec((1, d))]
    kern = functools.partial(_readout_kernel, mode=mode)
    return pl.pallas_call(
        kern,
        grid=(b, lay.tiles),
        in_specs=head_specs + [
            _full_spec((d, d)), _tok_spec(d, lay), _mod_spec(lay), _full_spec((1, d)),
            _full_spec((d, LANES)), _full_spec((d, LANES)), _full_spec((1, LANES)),
        ],
        out_specs=[_tok_spec(d, lay), pl.BlockSpec((1, TOKEN_TILE * ROW_TILE, LANES), lambda bb, t: (bb, t, 0)),
                   _tok_spec(LANES, lay), _tok_spec(LANES, lay),
                   _tok_spec(LANES, lay), _full_spec((SUBLANES, LANES))],
        out_shape=[
            jax.ShapeDtypeStruct((b, n, d), F32),
            jax.ShapeDtypeStruct((b, n * ROW_TILE, LANES), F32),
            jax.ShapeDtypeStruct((b, n, LANES), I32),
            jax.ShapeDtypeStruct((b, n, LANES), F32),
            jax.ShapeDtypeStruct((b, n, LANES), I32),
            jax.ShapeDtypeStruct((SUBLANES, LANES), F32),
        ],
        scratch_shapes=[pltpu.VMEM((SUBLANES, LANES), F32)],
        compiler_params=_cparams(("arbitrary", "arbitrary")),
        name="readout_" + mode,
    )(*head_in, w_out.astype(BF16), h, mod_l, norm2_g.reshape(1, d), wr_hi, wr_lo, br)


def _dest_kernel(e_ref, rank_ref, start_ref, dest_ref):
    pos = (rank_ref[0] + start_ref[...]).astype(F32)
    e = e_ref[0]
    lane = lax.broadcasted_iota(I32, e.shape, 1)
    dest = jnp.zeros(e.shape, F32)
    for jx in range(TOP_K):
        slot = jnp.sum(jnp.where(lane == e[:, jx:jx + 1], pos, 0.0), axis=-1, keepdims=True)
        dest = jnp.where(lane == jx, slot, dest)
    dest_ref[0] = dest.astype(I32)


def _dest(top_e, rank, pad_start, lay):
    b, n = lay.batch, lay.n
    return pl.pallas_call(
        _dest_kernel,
        grid=(b, lay.tiles),
        in_specs=[_tok_spec(LANES, lay), _tok_spec(LANES, lay), _full_spec((1, LANES))],
        out_specs=_tok_spec(LANES, lay),
        out_shape=jax.ShapeDtypeStruct((b, n, LANES), I32),
        compiler_params=_cparams(("parallel", "parallel")),
        name="moe_dest",
    )(top_e, rank, pad_start)


def _row_copies_wait(src_block, dst_rows, sem):
    for _ in range(TOP_K):
        pltpu.make_async_copy(src_block, dst_rows, sem).wait()


def _zero_fill(zs_ref, nu_ref, xb_ref, zbuf_ref, sem):
    zbuf_ref[...] = jnp.zeros_like(zbuf_ref)
    blk_rows = zbuf_ref.shape[0]
    n_blocks = xb_ref.shape[0] // blk_rows
    for e in range(N_EXPERTS):
        start = pl.multiple_of(zs_ref[e] * ROW_TILE, ROW_TILE)
        pltpu.make_async_copy(zbuf_ref, xb_ref.at[pl.ds(start, blk_rows), :], sem).start()
    for e in range(N_EXPERTS):
        pltpu.make_async_copy(zbuf_ref, xb_ref.at[pl.ds(0, blk_rows), :], sem).wait()
    for i in range(n_blocks - N_EXPERTS, n_blocks):

        @pl.when(i >= nu_ref[0])
        def _():
            cp = pltpu.make_async_copy(zbuf_ref, xb_ref.at[pl.ds(i * blk_rows, blk_rows), :], sem)
            cp.start()
            cp.wait()


def _dispatch_kernel(dest_ref, zs_ref, nu_ref, v_ref, xb_ref, zbuf_ref, sem):
    rows = v_ref.shape[1] // ROW_TILE

    @pl.when((pl.program_id(0) == 0) & (pl.program_id(1) == 0))
    def _():
        _zero_fill(zs_ref, nu_ref, xb_ref, zbuf_ref, sem)

    def body(r, carry):
        src = v_ref.at[0, pl.ds(pl.multiple_of(r * ROW_TILE, ROW_TILE), ROW_TILE), :]
        for jx in range(TOP_K):
            slot = dest_ref[0, 0, r * TOP_K + jx]
            dst = xb_ref.at[pl.ds(pl.multiple_of(slot * ROW_TILE, ROW_TILE), ROW_TILE), :]
            pltpu.make_async_copy(src, dst, sem).start()
        return carry

    lax.fori_loop(0, rows, body, 0, unroll=DMA_UNROLL)
    _row_copies_wait(v_ref.at[0], xb_ref.at[pl.ds(0, rows * ROW_TILE), :], sem)


def _dispatch(dest_c, zero_start, n_used, v_rows, cap, lay):
    tiles = lay.tiles
    return pl.pallas_call(
        _dispatch_kernel,
        grid=(lay.batch, tiles),
        in_specs=[
            pl.BlockSpec((1, 1, TOKEN_TILE * TOP_K), lambda bb, t: (bb * tiles + t, 0, 0),
                         memory_space=pltpu.SMEM),
            pl.BlockSpec(memory_space=pltpu.SMEM),
            pl.BlockSpec(memory_space=pltpu.SMEM),
            pl.BlockSpec((1, TOKEN_TILE * ROW_TILE, LANES), lambda bb, t: (bb, t, 0)),
        ],
        out_specs=pl.BlockSpec(memory_space=pl.ANY),
        out_shape=jax.ShapeDtypeStruct((cap * ROW_TILE, LANES), F32),
        scratch_shapes=[pltpu.VMEM((MOE_BLOCK * ROW_TILE, LANES), F32), pltpu.SemaphoreType.DMA(())],
        compiler_params=_cparams(("arbitrary", "arbitrary")),
        name="moe_dispatch",
    )(dest_c, zero_start, n_used, v_rows)


def _expert_kernel(be_ref, nu_ref, x_ref, wgu_ref, bgu_ref, wd_ref, bd_ref, y_ref, wgu_bf_ref, wd_bf_ref):
    i = pl.program_id(0)
    used = i < nu_ref[0]
    new_expert = (i == 0) | (be_ref[i] != be_ref[jnp.maximum(i - 1, 0)])

    @pl.when(new_expert)
    def _():
        wgu_bf_ref[...] = wgu_ref[0].astype(BF16)
        wd_bf_ref[...] = wd_ref[0].astype(BF16)

    @pl.when(jnp.logical_not(used))
    def _():
        y_ref[...] = jnp.zeros_like(y_ref)

    @pl.when(used)
    def _():
        x = jnp.concatenate([x_ref[pl.ds(c, MOE_BLOCK, stride=ROW_TILE), :] for c in range(ROW_TILE)], axis=1)
        gu = _dot(x.astype(BF16), wgu_bf_ref[...]) + bgu_ref[0]
        g = jnp.minimum(gu[:, :D_EXPERT], SWIGLU_LIMIT)
        lin = jnp.clip(gu[:, D_EXPERT:], -SWIGLU_LIMIT, SWIGLU_LIMIT)
        hdn = g * _sigmoid(SWIGLU_ALPHA * g) * (lin + 1.0)
        y = _dot(hdn.astype(BF16), wd_bf_ref[...]) + bd_ref[0]
        for c in range(ROW_TILE):
            y_ref[pl.ds(c, MOE_BLOCK, stride=ROW_TILE), :] = y[:, c * LANES:(c + 1) * LANES]


def _experts(block_e, n_used, xb, w_gu, b_gu, w_d, b_d):
    d, de = D_MODEL, D_EXPERT
    blk_rows = MOE_BLOCK * ROW_TILE
    n_blocks = xb.shape[0] // blk_rows

    def row_map(i, be, nu):
        return (jnp.minimum(i, nu[0] - 1), 0)

    def exp_map(i, be, nu):
        return (be[i], 0, 0)

    grid_spec = pltpu.PrefetchScalarGridSpec(
        num_scalar_prefetch=2,
        grid=(n_blocks,),
        in_specs=[
            pl.BlockSpec((blk_rows, LANES), row_map),
            pl.BlockSpec((1, d, 2 * de), exp_map),
            pl.BlockSpec((1, 1, 2 * de), exp_map),
            pl.BlockSpec((1, de, d), exp_map),
            pl.BlockSpec((1, 1, d), exp_map),
        ],
        out_specs=pl.BlockSpec((blk_rows, LANES), lambda i, be, nu: (i, 0)),
        scratch_shapes=[pltpu.VMEM((d, 2 * de), BF16), pltpu.VMEM((de, d), BF16)],
    )
    return pl.pallas_call(
        _expert_kernel,
        grid_spec=grid_spec,
        out_shape=jax.ShapeDtypeStruct(xb.shape, F32),
        compiler_params=_cparams(("arbitrary",), vmem=EXPERT_VMEM_LIMIT),
        name="moe_experts",
    )(block_e, n_used, xb, w_gu, b_gu.reshape(N_EXPERTS, 1, 2 * de), w_d, b_d.reshape(N_EXPERTS, 1, d))


def _combine_kernel(dest_ref, gate_ref, yb_ref, h_ref, mod_ref, out_ref, ybuf_ref, acc_ref, sem):
    rows = h_ref.shape[1]

    def issue(r, carry):
        base = pl.multiple_of(r * ROW_TILE, ROW_TILE)
        for jx in range(TOP_K):
            slot = dest_ref[0, 0, r * TOP_K + jx]
            src = yb_ref.at[pl.ds(pl.multiple_of(slot * ROW_TILE, ROW_TILE), ROW_TILE), :]
            pltpu.make_async_copy(src, ybuf_ref.at[jx, pl.ds(base, ROW_TILE), :], sem).start()
        return carry

    lax.fori_loop(0, rows, issue, 0, unroll=DMA_UNROLL)
    _row_copies_wait(yb_ref.at[pl.ds(0, rows * ROW_TILE), :], ybuf_ref.at[0], sem)

    def accumulate(r, carry):
        base = pl.multiple_of(r * ROW_TILE, ROW_TILE)
        a = gate_ref[0, 0, r * TOP_K] * ybuf_ref[0, pl.ds(base, ROW_TILE), :]
        for jx in range(1, TOP_K):
            a = a + gate_ref[0, 0, r * TOP_K + jx] * ybuf_ref[jx, pl.ds(base, ROW_TILE), :]
        acc_ref[pl.ds(base, ROW_TILE), :] = a
        return carry

    lax.fori_loop(0, rows, accumulate, 0, unroll=DMA_UNROLL)
    for c in range(ROW_TILE):
        cs = slice(c * LANES, (c + 1) * LANES)
        out_ref[0, :, cs] = h_ref[0, :, cs] + mod_ref[0, 5:6, cs] * acc_ref[pl.ds(c, rows, stride=ROW_TILE), :]


def _combine(dest_c, gate_c, yb, h, mod_l, lay):
    d = D_MODEL
    tiles = lay.tiles
    smem_spec = pl.BlockSpec((1, 1, TOKEN_TILE * TOP_K), lambda bb, t: (bb * tiles + t, 0, 0),
                             memory_space=pltpu.SMEM)
    return pl.pallas_call(
        _combine_kernel,
        grid=(lay.batch, tiles),
        in_specs=[smem_spec, smem_spec, pl.BlockSpec(memory_space=pl.ANY), _tok_spec(d, lay), _mod_spec(lay)],
        out_specs=_tok_spec(d, lay),
        out_shape=jax.ShapeDtypeStruct((lay.batch, lay.n, d), F32),
        scratch_shapes=[pltpu.VMEM((TOP_K, TOKEN_TILE * ROW_TILE, LANES), F32),
                        pltpu.VMEM((TOKEN_TILE * ROW_TILE, LANES), F32), pltpu.SemaphoreType.DMA(())],
        compiler_params=_cparams(("arbitrary", "arbitrary")),
        name="moe_combine",
    )(dest_c, gate_c, yb, h, mod_l)


def _moe(h, v_rows, top_e, gate, rank, counts, mod_l, w_gu, b_gu, w_d, b_d, lay):
    n_tokens = lay.batch * lay.n
    n_assign = n_tokens * TOP_K
    cap = (-(-n_assign // MOE_BLOCK) + N_EXPERTS) * MOE_BLOCK
    n_blocks = cap // MOE_BLOCK
    cnt = counts[0, :N_EXPERTS].astype(I32)
    padded = (cnt + MOE_BLOCK - 1) // MOE_BLOCK * MOE_BLOCK
    pad_ends = jnp.cumsum(padded)
    pad_starts = pad_ends - padded
    n_used = (pad_ends[-1] // MOE_BLOCK).astype(I32).reshape(1)
    blk = jnp.minimum(jnp.arange(n_blocks, dtype=I32), n_used[0] - 1) * MOE_BLOCK
    block_e = jnp.minimum(jnp.sum((pad_ends[None, :] <= blk[:, None]).astype(I32), axis=1), N_EXPERTS - 1)
    start_row = jnp.zeros((1, LANES), I32).at[0, :N_EXPERTS].set(pad_starts)
    zero_start = jnp.minimum(pad_starts + cnt, cap - MOE_BLOCK).astype(I32)

    dest = _dest(top_e, rank, start_row, lay)
    n_tiles = lay.batch * lay.tiles
    dest_c = dest[:, :, :TOP_K].reshape(n_tiles, 1, TOKEN_TILE * TOP_K)
    gate_c = gate[:, :, :TOP_K].reshape(n_tiles, 1, TOKEN_TILE * TOP_K)
    xb = _dispatch(dest_c, zero_start, n_used, v_rows, cap, lay)
    yb = _experts(block_e, n_used, xb, w_gu, b_gu, w_d, b_d)
    return _combine(dest_c, gate_c, yb, h, mod_l, lay)


def kernel(x, c, ctx, c_ctx, norm1_g, norm2_g, mod_w, mod_b,
           mlstm_w_in, mlstm_conv_w, mlstm_gate_b, mlstm_out_norm_g, mlstm_w_out,
           attn_w_in, attn_q_norm_g, attn_k_norm_g, attn_w_out,
           gla_w_in, gla_alpha_w2, gla_alpha_b, gla_out_norm_g, gla_w_out,
           router_w, router_b, moe_w_gu, moe_b_gu, moe_w_down, moe_b_down):
    batch, n_lat, d = x.shape
    n_ctx = ctx.shape[1]
    depth = norm1_g.shape[0]
    assert d == D_MODEL
    lay = _Layout(batch, n_ctx, n_lat)

    cond_rows = -(-(batch + 1) // SUBLANES) * SUBLANES
    cond = jnp.zeros((cond_rows, d), F32).at[:batch].set(c).at[batch].set(c_ctx)
    mod_all = _modulation(cond, mod_w, mod_b)

    h = jnp.concatenate([ctx, x], axis=1)
    for layer in range(depth):
        kind, j = layer % 3, layer // 3
        mod_l = mod_all[layer]
        if kind == 0:
            qk_pre, v_m, o_gate, gcol, grow = _mlstm_in(h, norm1_g[layer], mod_l, mlstm_w_in[j],
                                                        mlstm_gate_b[j], lay)
            h_dir = _mlstm_scan(qk_pre, v_m, gcol, grow, mlstm_conv_w[j], lay)
            mixer = ("mlstm", (h_dir, o_gate, mlstm_out_norm_g[j]), mlstm_w_out[j])
        elif kind == 1:
            q_a, k_a, v_a = _gqa_in(h, norm1_g[layer], mod_l, attn_w_in[j], attn_q_norm_g[j],
                                    attn_k_norm_g[j], lay)
            mixer = ("attn", (_attention(q_a, k_a, v_a, lay),), attn_w_out[j])
        else:
            q_g, k_g, v_g, g_gate, la_f, la_b = _gla_in(h, norm1_g[layer], mod_l, gla_w_in[j],
                                                        gla_alpha_w2[j], gla_alpha_b[j], lay)
            o_f = _gla_scan(q_g, k_g, v_g, la_f, lay, reverse=False)
            o_b = _gla_scan(q_g, k_g, v_g, la_b, lay, reverse=True)
            mixer = ("gla", (o_f, o_b, g_gate, gla_out_norm_g[j]), gla_w_out[j])
        mode, mixer_outs, w_out = mixer
        h, v_moe, top_e, gate, rank, counts = _readout(mode, mixer_outs, w_out, h, mod_l, norm2_g[layer],
                                                       router_w[layer], router_b[layer], lay)
        h = _moe(h, v_moe, top_e, gate, rank, counts, mod_l,
                 moe_w_gu[layer], moe_b_gu[layer], moe_w_down[layer], moe_b_down[layer], lay)
    return h[:, n_ctx:]
```

```python
import functools

import jax
import jax.numpy as jnp
from jax import lax
from jax.experimental import pallas as pl
from jax.experimental.pallas import tpu as pltpu

F32 = jnp.float32
BF16 = jnp.bfloat16
I32 = jnp.int32
HIGHEST = lax.Precision.HIGHEST

D_MODEL = 1024
GRID_W = 64
NORM_EPS = 1e-6

MLSTM_HEADS = 8
MLSTM_DQK = 64
MLSTM_DV = 128
MLSTM_QK = MLSTM_HEADS * MLSTM_DQK

ATTN_HEADS = 8
ATTN_KV_HEADS = 4
ATTN_GROUP = ATTN_HEADS // ATTN_KV_HEADS
ATTN_HEAD_DIM = 128
ROPE_THETA = 10000.0

GLA_HEADS = 4
GLA_DK = 128
GLA_DV = 256
GLA_GATE_RANK = 16
GLA_TAU = 16.0
GLA_QK = GLA_HEADS * GLA_DK

N_EXPERTS = 32
TOP_K = 4
D_EXPERT = D_MODEL
SWIGLU_LIMIT = 7.0
SWIGLU_ALPHA = 1.702

LANES = 128
SUBLANES = 8
TOKEN_TILE = 256
MLSTM_CHUNK = 128
GLA_TILE = 128
GLA_CHUNK = 32
ATTN_KV_CHUNK = 512
ATTN_Q_SCALE = ATTN_HEAD_DIM ** -0.5 * 1.4426950408889634
MOE_BLOCK = 256
ROW_TILE = D_MODEL // LANES
assert ROW_TILE == SUBLANES
DMA_UNROLL = 8
DMA_THREADS = 2
VMEM_LIMIT = 48 * 1024 * 1024
EXPERT_VMEM_LIMIT = 56 * 1024 * 1024
NEG_BIG = -1e30


def _cparams(semantics, vmem=VMEM_LIMIT):
    return pltpu.CompilerParams(dimension_semantics=semantics, vmem_limit_bytes=vmem)


def _log_sigmoid(x):
    return jnp.minimum(x, 0.0) - jnp.log(1.0 + jnp.exp(-jnp.abs(x)))


def _sigmoid(x):
    return 1.0 / (1.0 + jnp.exp(-x))


def _norm_mod(x, g, shift, scale):
    ms = jnp.mean(x * x, axis=-1, keepdims=True)
    y = x * lax.rsqrt(ms + NORM_EPS) * g
    return y * (1.0 + scale) + shift


def _head_rms(x, head_dim):
    outs = []
    for i in range(x.shape[1] // head_dim):
        xs = x[:, i * head_dim:(i + 1) * head_dim]
        ms = jnp.mean(xs * xs, axis=-1, keepdims=True)
        outs.append(xs * lax.rsqrt(ms + NORM_EPS))
    return jnp.concatenate(outs, axis=1)


def _dot(a, b):
    return jnp.dot(a, b, preferred_element_type=F32)


def _dot_nt(a, b):
    return lax.dot_general(a, b, (((1,), (1,)), ((), ())), preferred_element_type=F32)


def _dot_tn(a, b):
    return lax.dot_general(a, b, (((0,), (0,)), ((), ())), preferred_element_type=F32)


def _mod_kernel(c_ref, w_ref, b_ref, o_ref):
    x = c_ref[...]
    x = x * _sigmoid(x)
    o_ref[0, 0] = _dot(x.astype(BF16), w_ref[0].astype(BF16)) + b_ref[0, 0]


def _modulation(cond_rows, mod_w, mod_b):
    depth, d, _ = mod_w.shape
    rows = cond_rows.shape[0]
    out = pl.pallas_call(
        _mod_kernel,
        grid=(depth, 6),
        in_specs=[
            pl.BlockSpec((rows, d), lambda l, j: (0, 0)),
            pl.BlockSpec((1, d, d), lambda l, j: (l, 0, j)),
            pl.BlockSpec((1, 1, 1, d), lambda l, j: (l, j, 0, 0)),
        ],
        out_specs=pl.BlockSpec((1, 1, rows, d), lambda l, j: (l, j, 0, 0)),
        out_shape=jax.ShapeDtypeStruct((depth, 6, rows, d), F32),
        compiler_params=_cparams(("parallel", "parallel")),
        name="modulation",
    )(cond_rows, mod_w, mod_b.reshape(depth, 6, 1, d))
    return out.transpose(0, 2, 1, 3)


class _Layout:
    def __init__(self, batch, n_ctx, n_lat):
        self.batch = batch
        self.n_ctx = n_ctx
        self.n_lat = n_lat
        self.n = n_ctx + n_lat
        assert n_ctx % TOKEN_TILE == 0 and n_lat % TOKEN_TILE == 0
        assert n_ctx % MLSTM_CHUNK == 0 and n_lat % MLSTM_CHUNK == 0
        assert n_ctx % GLA_TILE == 0 and n_lat % GLA_TILE == 0
        self.tiles = self.n // TOKEN_TILE
        self.ctx_tiles = n_ctx // TOKEN_TILE
        self.ctx_row = batch

    def mod_index(self, b, t):
        return jnp.where(t < self.ctx_tiles, self.ctx_row, b)


def _tok_spec(width, lay):
    return pl.BlockSpec((1, TOKEN_TILE, width), lambda b, t: (b, t, 0))


def _feat_spec(features):
    return pl.BlockSpec((1, features, TOKEN_TILE), lambda b, t: (b, 0, t))


def _full_spec(shape):
    nd = len(shape)
    return pl.BlockSpec(shape, lambda b, t: (0,) * nd)


def _mod_spec(lay):
    return pl.BlockSpec((1, 6, D_MODEL), lambda b, t: (lay.mod_index(b, t), 0, 0))


def _mlstm_in_kernel(h_ref, g_ref, mod_ref, w_ref, wvot_ref, wg_ref, wgt_ref, gb_ref, gbt_ref,
                     qk_ref, vt_ref, ot_ref, gc_ref, gr_ref):
    mod = mod_ref[0]
    u = _norm_mod(h_ref[0], g_ref[...], mod[0:1], mod[1:2]).astype(BF16)
    qk_ref[0] = _dot(u, w_ref[...])
    vo_t = _dot_nt(wvot_ref[...], u)
    vt_ref[0] = vo_t[:D_MODEL].astype(BF16)
    ot_ref[0] = vo_t[D_MODEL:]
    gc = _dot(u, wg_ref[...]) + gb_ref[...]
    lane = lax.broadcasted_iota(I32, gc.shape, 1)
    gc_ref[0] = jnp.where(((lane >> 3) & 1) == 1, _log_sigmoid(gc), gc)
    gr = _dot_nt(wgt_ref[...], u) + gbt_ref[...]
    sub = lax.broadcasted_iota(I32, gr.shape, 0)
    gr_ref[0] = jnp.where(((sub >> 3) & 1) == 1, _log_sigmoid(gr), gr)


def _mlstm_in(h, norm_g, mod_l, w_in, gate_b, lay):
    d = D_MODEL
    n_qk = 2 * MLSTM_QK
    n_main = n_qk + 2 * d
    n_gate = 4 * MLSTM_HEADS
    w_qk = w_in[:, :n_qk].astype(BF16)
    w_vo_t = w_in[:, n_qk:n_main].T.astype(BF16)
    w_gate = w_in[:, n_main:]
    wg = jnp.zeros((d, LANES), F32).at[:, :n_gate].set(w_gate).astype(BF16)
    wgt = w_gate.T.astype(BF16)
    gb = jnp.zeros((1, LANES), F32).at[0, :n_gate].set(gate_b.reshape(-1))
    gbt = gate_b.reshape(n_gate, 1)
    b, n = lay.batch, lay.n
    return pl.pallas_call(
        _mlstm_in_kernel,
        grid=(b, lay.tiles),
        in_specs=[
            _tok_spec(d, lay), _full_spec((1, d)), _mod_spec(lay),
            _full_spec((d, n_qk)), _full_spec((2 * d, d)), _full_spec((d, LANES)), _full_spec((n_gate, d)),
            _full_spec((1, LANES)), _full_spec((n_gate, 1)),
        ],
        out_specs=[
            _tok_spec(n_qk, lay), _feat_spec(d), _feat_spec(d), _tok_spec(LANES, lay), _feat_spec(n_gate),
        ],
        out_shape=[
            jax.ShapeDtypeStruct((b, n, n_qk), F32),
            jax.ShapeDtypeStruct((b, d, n), BF16),
            jax.ShapeDtypeStruct((b, d, n), F32),
            jax.ShapeDtypeStruct((b, n, LANES), F32),
            jax.ShapeDtypeStruct((b, n_gate, n), F32),
        ],
        compiler_params=_cparams(("parallel", "parallel")),
        name="mlstm_in",
    )(h, norm_g.reshape(1, d), mod_l, w_qk, w_vo_t, wg, wgt, gb, gbt)


def _mlstm_conv_kernel(x_ref, xp_ref, xn_ref, cw_ref, qt_ref, k_ref, *, tiles, ctx_tiles):
    t = pl.program_id(1)
    x = x_ref[0]
    rows = x.shape[0]
    seg_first = (t == 0) | (t == ctx_tiles)
    seg_last = (t == ctx_tiles - 1) | (t == tiles - 1)
    prev_row = jnp.where(seg_first, 0.0, xp_ref[0, SUBLANES - 1:SUBLANES, :])
    next_row = jnp.where(seg_last, 0.0, xn_ref[0, 0:1, :])
    rid = lax.broadcasted_iota(I32, x.shape, 0)
    x_m1 = jnp.where(rid == 0, prev_row, pltpu.roll(x, 1, 0))
    x_p1 = jnp.where(rid == rows - 1, next_row, pltpu.roll(x, rows - 1, 0))
    cw = cw_ref[...]
    y = cw[0:1] * x_m1 + cw[1:2] * x + cw[2:3] * x_p1
    qk = y * _sigmoid(y)
    qt_ref[0] = (qk[:, :MLSTM_QK] * MLSTM_DQK ** -0.5).T.astype(BF16)
    k_ref[0] = qk[:, MLSTM_QK:].astype(BF16)


def _mlstm_conv(qk_pre, conv_w, lay):
    b, n = lay.batch, lay.n
    n_qk = 2 * MLSTM_QK
    halo = TOKEN_TILE // SUBLANES
    n_halo = n // SUBLANES
    kern = functools.partial(_mlstm_conv_kernel, tiles=lay.tiles, ctx_tiles=lay.ctx_tiles)
    return pl.pallas_call(
        kern,
        grid=(b, lay.tiles),
        in_specs=[
            _tok_spec(n_qk, lay),
            pl.BlockSpec((1, SUBLANES, n_qk), lambda bb, t: (bb, jnp.maximum(t * halo - 1, 0), 0)),
            pl.BlockSpec((1, SUBLANES, n_qk), lambda bb, t: (bb, jnp.minimum((t + 1) * halo, n_halo - 1), 0)),
            _full_spec((3, n_qk)),
        ],
        out_specs=[_feat_spec(MLSTM_QK), _tok_spec(MLSTM_QK, lay)],
        out_shape=[
            jax.ShapeDtypeStruct((b, MLSTM_QK, n), BF16),
            jax.ShapeDtypeStruct((b, n, MLSTM_QK), BF16),
        ],
        compiler_params=_cparams(("parallel", "parallel")),
        name="mlstm_conv",
    )(qk_pre, qk_pre, qk_pre, conv_w)


def _scan_chunk_index(reverse, j, n_chunks, ctx_chunks):
    back = jnp.where(j < ctx_chunks, ctx_chunks - 1 - j, n_chunks - 1 - (j - ctx_chunks))
    return jnp.where(reverse, back, j)


def _mlstm_scan_kernel(qt_ref, k_ref, vt_ref, gc_ref, gr_ref, out_ref, st_ref, m_ref):
    L = MLSTM_CHUNK
    dk, dv, nh = MLSTM_DQK, MLSTM_DV, MLSTM_HEADS
    d = pl.program_id(1)
    j = pl.program_id(2)

    @pl.when(j == 0)
    def _():
        st_ref[...] = jnp.zeros_like(st_ref)
        m_ref[...] = jnp.zeros_like(m_ref)

    q_t = qt_ref[0]
    k = k_ref[0]
    v_t = vt_ref[0]

    gcol = gc_ref[0]
    gcol = jnp.where(d == 0, gcol, pltpu.roll(gcol, LANES - 2 * nh, 1))
    grow = gr_ref[0, pl.ds(pl.multiple_of(d * 2 * nh, 2 * nh), 2 * nh), :]

    row = lax.broadcasted_iota(I32, (L, L), 0)
    col = lax.broadcasted_iota(I32, (L, L), 1)
    sgn = 1 - 2 * d
    mask_t = sgn * (col - row) >= 0
    cum_rows = jnp.dot(grow, mask_t.astype(F32), precision=HIGHEST, preferred_element_type=F32)
    cum_cols = jnp.dot((sgn * (row - col) >= 0).astype(F32), gcol, precision=HIGHEST,
                       preferred_element_type=F32)
    b_minus_i = cum_cols - pltpu.roll(gcol, nh, 1)
    b_tot = jnp.sum(grow[nh:], axis=-1, keepdims=True)

    ones_row = (lax.broadcasted_iota(I32, (LANES, L), 0) == 0).astype(BF16)
    for h in range(nh):
        b_t = cum_rows[nh + h:nh + h + 1, :]
        li = grow[h:h + 1, :]
        m_prev = m_ref[h, 0:1, 0:1]
        dmat = jnp.where(mask_t, b_t - b_minus_i[:, nh + h:nh + h + 1], -jnp.inf)
        inter = b_t + m_prev
        m_t = jnp.maximum(inter, jnp.max(dmat, axis=0, keepdims=True))
        w = jnp.exp(dmat - m_t)
        a_inter = jnp.exp(inter - m_t)
        qt_h = q_t[h * dk:(h + 1) * dk, :]
        k_h = k[:, h * dk:(h + 1) * dk]
        s = (_dot(k_h, qt_h) * w).astype(BF16)
        vt_aug = jnp.concatenate([v_t[h * dv:(h + 1) * dv, :], ones_row], axis=0)
        st = st_ref[h]
        r = a_inter * _dot(st.astype(BF16), qt_h) + _dot(vt_aug, s)
        den = r[dv:dv + 1, :]
        out_ref[0, 0, h * dv:(h + 1) * dv, :] = r[:dv] / jnp.maximum(jnp.abs(den), jnp.exp(-m_t))
        b_last = b_tot[h:h + 1, :]
        g = b_last - b_t + li
        m_new = jnp.maximum(b_last + m_prev, jnp.max(g, axis=-1, keepdims=True))
        decay = jnp.exp(b_last + m_prev - m_new)
        wg = jnp.exp(g - m_new)
        st_ref[h] = decay * st + _dot((vt_aug.astype(F32) * wg).astype(BF16), k_h)
        m_ref[h] = jnp.broadcast_to(m_new, (SUBLANES, LANES))


def _mlstm_scan(q_t, k, v_t, gcol, grow, lay):
    b, n = lay.batch, lay.n
    L = MLSTM_CHUNK
    n_chunks, ctx_chunks = n // L, lay.n_ctx // L

    def cidx(d, j):
        return _scan_chunk_index(d == 1, j, n_chunks, ctx_chunks)

    return pl.pallas_call(
        _mlstm_scan_kernel,
        grid=(b, 2, n_chunks),
        in_specs=[
            pl.BlockSpec((1, MLSTM_QK, L), lambda bb, d, j: (bb, 0, cidx(d, j))),
            pl.BlockSpec((1, L, MLSTM_QK), lambda bb, d, j: (bb, cidx(d, j), 0)),
            pl.BlockSpec((1, D_MODEL, L), lambda bb, d, j: (bb, 0, cidx(d, j))),
            pl.BlockSpec((1, L, LANES), lambda bb, d, j: (bb, cidx(d, j), 0)),
            pl.BlockSpec((1, 4 * MLSTM_HEADS, L), lambda bb, d, j: (bb, 0, cidx(d, j))),
        ],
        out_specs=pl.BlockSpec((1, 1, D_MODEL, L), lambda bb, d, j: (d, bb, 0, cidx(d, j))),
        out_shape=jax.ShapeDtypeStruct((2, b, D_MODEL, n), F32),
        scratch_shapes=[
            pltpu.VMEM((MLSTM_HEADS, MLSTM_DV + LANES, MLSTM_DQK), F32),
            pltpu.VMEM((MLSTM_HEADS, SUBLANES, LANES), F32),
        ],
        compiler_params=_cparams(("parallel", "parallel", "arbitrary")),
        name="mlstm_scan",
    )(q_t, k, v_t, gcol, grow)


def _gqa_in_kernel(h_ref, g_ref, mod_ref, w_ref, qg_ref, kg_ref, cos_ref, sin_ref, q_ref, k_ref, v_ref):
    hd = ATTN_HEAD_DIM
    mod = mod_ref[0]
    u = _norm_mod(h_ref[0], g_ref[...], mod[0:1], mod[1:2]).astype(BF16)
    p = _dot(u, w_ref[...])
    cos = cos_ref[...]
    sin = sin_ref[...]
    lane = lax.broadcasted_iota(I32, cos.shape, 1)
    first_half = (lane & (hd // 4)) == 0

    def rope(xh):
        swapped = jnp.where(first_half, pltpu.roll(xh, hd - hd // 4, 1), pltpu.roll(xh, hd // 4, 1))
        return xh * cos + swapped * sin

    nq, nk = ATTN_HEADS * hd, ATTN_KV_HEADS * hd
    qn = _head_rms(p[:, :nq], hd)
    kn = _head_rms(p[:, nq:nq + nk], hd)
    qg = qg_ref[...]
    kg = kg_ref[...]
    for i in range(ATTN_HEADS):
        q_ref[0, :, i * hd:(i + 1) * hd] = (rope(qn[:, i * hd:(i + 1) * hd] * qg) * ATTN_Q_SCALE).astype(BF16)
    for i in range(ATTN_KV_HEADS):
        k_ref[0, :, i * hd:(i + 1) * hd] = rope(kn[:, i * hd:(i + 1) * hd] * kg).astype(BF16)
    v_ref[0] = p[:, nq + nk:].astype(BF16)


def _rope_tables(lay):
    hd = ATTN_HEAD_DIM
    quarter = hd // 4
    inv = ROPE_THETA ** (-jnp.arange(quarter, dtype=F32) / quarter)
    pos = jnp.arange(lay.n_lat)
    rows = (pos // GRID_W).astype(F32)
    cols = (pos % GRID_W).astype(F32)
    ang = jnp.concatenate([jnp.tile(rows[:, None] * inv, (1, 2)), jnp.tile(cols[:, None] * inv, (1, 2))], axis=1)
    sign = jnp.tile(jnp.concatenate([-jnp.ones(quarter, F32), jnp.ones(quarter, F32)]), 2)
    cos = jnp.concatenate([jnp.ones((lay.n_ctx, hd), F32), jnp.cos(ang)], axis=0)
    sin = jnp.concatenate([jnp.zeros((lay.n_ctx, hd), F32), jnp.sin(ang) * sign], axis=0)
    return cos, sin


def _gqa_in(h, norm_g, mod_l, w_in, q_g, k_g, lay):
    d, hd = D_MODEL, ATTN_HEAD_DIM
    nq, nk = ATTN_HEADS * hd, ATTN_KV_HEADS * hd
    cos, sin = _rope_tables(lay)
    b, n = lay.batch, lay.n
    tab_spec = pl.BlockSpec((TOKEN_TILE, hd), lambda bb, t: (t, 0))
    return pl.pallas_call(
        _gqa_in_kernel,
        grid=(b, lay.tiles),
        in_specs=[
            _tok_spec(d, lay), _full_spec((1, d)), _mod_spec(lay), _full_spec((d, nq + 2 * nk)),
            _full_spec((1, hd)), _full_spec((1, hd)), tab_spec, tab_spec,
        ],
        out_specs=[_tok_spec(nq, lay), _tok_spec(nk, lay), _tok_spec(nk, lay)],
        out_shape=[
            jax.ShapeDtypeStruct((b, n, nq), BF16),
            jax.ShapeDtypeStruct((b, n, nk), BF16),
            jax.ShapeDtypeStruct((b, n, nk), BF16),
        ],
        compiler_params=_cparams(("parallel", "parallel")),
        name="gqa_in",
    )(h, norm_g.reshape(1, d), mod_l, w_in.astype(BF16), q_g.reshape(1, hd), k_g.reshape(1, hd), cos, sin)


def _attn_kernel(q_ref, k_ref, v_ref, o_ref, *, ctx_tiles, n_ctx, n_all):
    hd = ATTN_HEAD_DIM
    t = pl.program_id(2)
    q = q_ref[0]
    rows = q.shape[0]
    q2 = jnp.concatenate([q[:, g * hd:(g + 1) * hd] for g in range(ATTN_GROUP)], axis=0)

    def attend(n_keys):
        bounds = [0, n_ctx] + list(range(n_ctx + ATTN_KV_CHUNK, n_keys + 1, ATTN_KV_CHUNK))
        assert bounds[-1] == n_keys
        m = l = acc = None
        for lo, hi in zip(bounds[:-1], bounds[1:]):
            s = _dot_nt(q2, k_ref[0, lo:hi, :])
            mc = jnp.max(s, axis=-1, keepdims=True)
            if m is None:
                m_new = mc
                p = jnp.exp2(s - m_new)
                l = jnp.sum(p, axis=-1, keepdims=True)
                acc = _dot(p.astype(BF16), v_ref[0, lo:hi, :])
            else:
                m_new = jnp.maximum(m, mc)
                alpha = jnp.exp2(m - m_new)
                p = jnp.exp2(s - m_new)
                l = alpha * l + jnp.sum(p, axis=-1, keepdims=True)
                acc = alpha * acc + _dot(p.astype(BF16), v_ref[0, lo:hi, :])
            m = m_new
        o = acc / l
        for g in range(ATTN_GROUP):
            o_ref[0, :, g * hd:(g + 1) * hd] = o[g * rows:(g + 1) * rows].astype(BF16)

    @pl.when(t < ctx_tiles)
    def _():
        attend(n_ctx)

    @pl.when(t >= ctx_tiles)
    def _():
        attend(n_all)


def _attention(q, k, v, lay):
    hd = ATTN_HEAD_DIM
    b, n = lay.batch, lay.n
    gw = ATTN_GROUP * hd
    kern = functools.partial(_attn_kernel, ctx_tiles=lay.ctx_tiles, n_ctx=lay.n_ctx, n_all=n)
    return pl.pallas_call(
        kern,
        grid=(b, ATTN_KV_HEADS, lay.tiles),
        in_specs=[
            pl.BlockSpec((1, TOKEN_TILE, gw), lambda bb, kh, t: (bb, t, kh)),
            pl.BlockSpec((1, n, hd), lambda bb, kh, t: (bb, 0, kh)),
            pl.BlockSpec((1, n, hd), lambda bb, kh, t: (bb, 0, kh)),
        ],
        out_specs=pl.BlockSpec((1, TOKEN_TILE, gw), lambda bb, kh, t: (bb, t, kh)),
        out_shape=jax.ShapeDtypeStruct((b, n, ATTN_HEADS * hd), BF16),
        compiler_params=_cparams(("parallel", "parallel", "parallel")),
        name="attention",
    )(q, k, v)


def _gla_in_kernel(h_ref, g_ref, mod_ref, w_ref, wa_ref, aw_ref, ab_ref,
                   q_ref, k_ref, v_ref, gate_ref, laf_ref, lab_ref):
    d = D_MODEL
    mod = mod_ref[0]
    u = _norm_mod(h_ref[0], g_ref[...], mod[0:1], mod[1:2]).astype(BF16)
    p = _dot(u, w_ref[...])
    q_ref[0] = p[:, :GLA_QK] * GLA_DK ** -0.5
    k_ref[0] = p[:, GLA_QK:2 * GLA_QK]
    v_ref[0] = p[:, 2 * GLA_QK:2 * GLA_QK + d].astype(BF16)
    gate_ref[0] = p[:, 2 * GLA_QK + d:]
    a_low = _dot(u, wa_ref[...]).astype(BF16)
    pre = _dot(a_low, aw_ref[...]) + ab_ref[...]
    la = _log_sigmoid(pre) * (1.0 / GLA_TAU)
    laf_ref[0] = la[:, :GLA_QK]
    lab_ref[0] = la[:, GLA_QK:]


def _gla_in(h, norm_g, mod_l, w_in, alpha_w2, alpha_b, lay):
    d, r = D_MODEL, GLA_GATE_RANK
    n_main = 2 * GLA_QK + 2 * d
    w_main = w_in[:, :n_main].astype(BF16)
    wa = jnp.zeros((d, LANES), F32).at[:, :2 * r].set(w_in[:, n_main:]).astype(BF16)
    aw = jnp.zeros((LANES, 2 * GLA_QK), F32)
    aw = aw.at[:r, :GLA_QK].set(alpha_w2[0]).at[r:2 * r, GLA_QK:].set(alpha_w2[1]).astype(BF16)
    ab = alpha_b.reshape(1, 2 * GLA_QK)
    b, n = lay.batch, lay.n
    return pl.pallas_call(
        _gla_in_kernel,
        grid=(b, lay.tiles),
        in_specs=[
            _tok_spec(d, lay), _full_spec((1, d)), _mod_spec(lay), _full_spec((d, n_main)),
            _full_spec((d, LANES)), _full_spec((LANES, 2 * GLA_QK)), _full_spec((1, 2 * GLA_QK)),
        ],
        out_specs=[_tok_spec(GLA_QK, lay), _tok_spec(GLA_QK, lay), _tok_spec(d, lay), _tok_spec(d, lay),
                   _tok_spec(GLA_QK, lay), _tok_spec(GLA_QK, lay)],
        out_shape=[
            jax.ShapeDtypeStruct((b, n, GLA_QK), F32),
            jax.ShapeDtypeStruct((b, n, GLA_QK), F32),
            jax.ShapeDtypeStruct((b, n, d), BF16),
            jax.ShapeDtypeStruct((b, n, d), F32),
            jax.ShapeDtypeStruct((b, n, GLA_QK), F32),
            jax.ShapeDtypeStruct((b, n, GLA_QK), F32),
        ],
        compiler_params=_cparams(("parallel", "parallel")),
        name="gla_in",
    )(h, norm_g.reshape(1, d), mod_l, w_main, wa, aw, ab)


def _gla_scan_kernel(q_ref, k_ref, v_ref, la_ref, out_ref, st_ref, *, reverse):
    T, C = GLA_TILE, GLA_CHUNK
    dk, dv, nh = GLA_DK, GLA_DV, GLA_HEADS
    j = pl.program_id(1)

    @pl.when(j == 0)
    def _():
        st_ref[...] = jnp.zeros_like(st_ref)

    row = lax.broadcasted_iota(I32, (T, T), 0)
    col = lax.broadcasted_iota(I32, (T, T), 1)
    same_chunk = (row // C) == (col // C)
    order = (col >= row) if reverse else (col <= row)
    cum_mask = (same_chunk & order).astype(F32)
    b_all = jnp.dot(cum_mask, la_ref[0], precision=HIGHEST, preferred_element_type=F32)
    crow = lax.broadcasted_iota(I32, (C, C), 0)
    ccol = lax.broadcasted_iota(I32, (C, C), 1)
    causal = (ccol >= crow) if reverse else (ccol <= crow)

    chunks = range(T // C)
    for c in (reversed(chunks) if reverse else chunks):
        lo = c * C
        b = b_all[lo:lo + C]
        b_end = b[0:1] if reverse else b[C - 1:C]
        b_mid = b[C // 2:C // 2 + 1]
        qc = q_ref[0, lo:lo + C, :]
        kc = k_ref[0, lo:lo + C, :]
        q_state = (qc * jnp.exp(b)).astype(BF16)
        k_state = (kc * jnp.exp(b_end - b)).astype(BF16)
        q_mid = (qc * jnp.exp(b - b_mid)).astype(BF16)
        k_mid = (kc * jnp.exp(b_mid - b)).astype(BF16)
        decay = jnp.exp(b_end)
        for h in range(nh):
            ks = slice(h * dk, (h + 1) * dk)
            vs = slice(h * dv, (h + 1) * dv)
            v_h = v_ref[0, lo:lo + C, vs]
            a = jnp.where(causal, _dot_nt(q_mid[:, ks], k_mid[:, ks]), 0.0)
            st = st_ref[h]
            out_ref[0, lo:lo + C, vs] = _dot_nt(q_state[:, ks], st.astype(BF16)) + _dot(a.astype(BF16), v_h)
            st_ref[h] = st * decay[:, ks] + _dot_tn(v_h, k_state[:, ks])


def _gla_scan(q, k, v, la, lay, reverse):
    b, n = lay.batch, lay.n
    n_tiles, ctx_tiles = n // GLA_TILE, lay.n_ctx // GLA_TILE

    def tidx(j):
        return _scan_chunk_index(reverse, j, n_tiles, ctx_tiles)

    kern = functools.partial(_gla_scan_kernel, reverse=reverse)
    return pl.pallas_call(
        kern,
        grid=(b, n_tiles),
        in_specs=[
            pl.BlockSpec((1, GLA_TILE, GLA_QK), lambda bb, j: (bb, tidx(j), 0)),
            pl.BlockSpec((1, GLA_TILE, GLA_QK), lambda bb, j: (bb, tidx(j), 0)),
            pl.BlockSpec((1, GLA_TILE, D_MODEL), lambda bb, j: (bb, tidx(j), 0)),
            pl.BlockSpec((1, GLA_TILE, GLA_QK), lambda bb, j: (bb, tidx(j), 0)),
        ],
        out_specs=pl.BlockSpec((1, GLA_TILE, D_MODEL), lambda bb, j: (bb, tidx(j), 0)),
        out_shape=jax.ShapeDtypeStruct((b, n, D_MODEL), F32),
        scratch_shapes=[pltpu.VMEM((GLA_HEADS, GLA_DV, GLA_DK), F32)],
        compiler_params=_cparams(("parallel", "arbitrary")),
        name="gla_scan_bwd" if reverse else "gla_scan_fwd",
    )(q, k, v, la)


def _readout_kernel(*refs, mode):
    if mode == "attn":
        a_ref, wout_ref = refs[:2]
        rest = refs[2:]
        y_in = a_ref[0]
    else:
        a_ref, b_ref, gsrc_ref, ng_ref, wout_ref = refs[:5]
        rest = refs[5:]
        hs = a_ref[0] + b_ref[0]
        if mode == "mlstm":
            normed = []
            for i in range(MLSTM_HEADS):
                xs = hs[i * MLSTM_DV:(i + 1) * MLSTM_DV, :]
                ms = jnp.mean(xs * xs, axis=0, keepdims=True)
                normed.append(xs * lax.rsqrt(ms + NORM_EPS))
            y_in = (jnp.concatenate(normed, axis=0) * ng_ref[...] * _sigmoid(gsrc_ref[0])).T
        else:
            gsrc = gsrc_ref[0]
            y_in = _head_rms(hs, GLA_DV) * ng_ref[...] * (gsrc * _sigmoid(gsrc))
        y_in = y_in.astype(BF16)
    (h_ref, mod_ref, n2g_ref, wrh_ref, wrl_ref, br_ref,
     hnew_ref, v_ref, e_ref, gate_ref, rank_ref, cnt_ref, carry_ref) = rest

    first = (pl.program_id(0) == 0) & (pl.program_id(1) == 0)

    @pl.when(first)
    def _():
        carry_ref[...] = jnp.zeros_like(carry_ref)

    mod = mod_ref[0]
    hn = h_ref[0] + mod[2:3] * _dot(y_in, wout_ref[...])
    hnew_ref[0] = hn
    v = _norm_mod(hn, n2g_ref[...], mod[3:4], mod[4:5])
    for c in range(ROW_TILE):
        v_ref[0, pl.ds(c, v.shape[0], stride=ROW_TILE), :] = v[:, c * LANES:(c + 1) * LANES]

    v_hi = v.astype(BF16)
    v_lo = (v - v_hi.astype(F32)).astype(BF16)
    w_hi = wrh_ref[...]
    logits = _dot(v_hi, w_hi) + _dot(v_lo, w_hi) + _dot(v_hi, wrl_ref[...]) + br_ref[...]

    tm = logits.shape[0]
    lane = lax.broadcasted_iota(I32, logits.shape, 1)
    lane_f = lane.astype(F32)
    work = logits
    top_e = jnp.zeros(logits.shape, F32)
    top_p = jnp.zeros(logits.shape, F32)
    onehot = jnp.zeros(logits.shape, F32)
    m0 = None
    for jx in range(TOP_K):
        mx = jnp.max(work, axis=-1, keepdims=True)
        idx = jnp.min(jnp.where(work == mx, lane_f, float(LANES)), axis=-1, keepdims=True)
        if jx == 0:
            m0 = mx
        hit = lane_f == idx
        top_e = jnp.where(lane == jx, idx, top_e)
        top_p = jnp.where(lane == jx, jnp.exp(mx - m0), top_p)
        onehot = jnp.where(hit, 1.0, onehot)
        work = jnp.where(hit, -jnp.inf, work)
    e_ref[0] = top_e.astype(I32)
    gate_ref[0] = top_p / jnp.sum(top_p, axis=-1, keepdims=True)

    r = lax.broadcasted_iota(I32, (tm, tm), 0)
    c = lax.broadcasted_iota(I32, (tm, tm), 1)
    tril = (c <= r).astype(BF16)
    cum = _dot(tril, onehot.astype(BF16))
    carry = carry_ref[0:1, :]
    rank_ref[0] = (carry + cum - onehot).astype(I32)
    total = carry + cum[tm - 1:tm, :]
    carry_ref[...] = jnp.broadcast_to(total, carry_ref.shape)
    cnt_ref[...] = jnp.broadcast_to(total, cnt_ref.shape)


def _readout(mode, mixer_outs, w_out, h, mod_l, norm2_g, router_w, router_b, lay):
    d = D_MODEL
    b, n = lay.batch, lay.n
    wr = jnp.zeros((d, LANES), F32).at[:, :N_EXPERTS].set(router_w)
    wr_hi = wr.astype(BF16)
    wr_lo = (wr - wr_hi.astype(F32)).astype(BF16)
    br = jnp.full((1, LANES), NEG_BIG, F32).at[0, :N_EXPERTS].set(router_b)
    if mode == "attn":
        (attn_o,) = mixer_outs
        head_in = [attn_o]
        head_specs = [_tok_spec(d, lay)]
    elif mode == "mlstm":
        h_dir, o_gate, norm_g = mixer_outs
        head_in = [h_dir, h_dir, o_gate, norm_g.reshape(d, 1)]
        head_specs = [pl.BlockSpec((None, 1, d, TOKEN_TILE), lambda bb, t: (0, bb, 0, t)),
                      pl.BlockSpec((None, 1, d, TOKEN_TILE), lambda bb, t: (1, bb, 0, t)),
                      _feat_spec(d), _full_spec((d, 1))]
    else:
        o_f, o_b, g_gate, norm_g = mixer_outs
        head_in = [o_f, o_b, g_gate, norm_g.reshape(1, d)]
        head_specs = [_tok_spec(d, lay), _tok_spec(d, lay), _tok_spec(d, lay), _full_spec((1, d))]
    kern = functools.partial(_readout_kernel, mode=mode)
    return pl.pallas_call(
        kern,
        grid=(b, lay.tiles),
        in_specs=head_specs + [
            _full_spec((d, d)), _tok_spec(d, lay), _mod_spec(lay), _full_spec((1, d)),
            _full_spec((d, LANES)), _full_spec((d, LANES)), _full_spec((1, LANES)),
        ],
        out_specs=[_tok_spec(d, lay), pl.BlockSpec((1, TOKEN_TILE * ROW_TILE, LANES), lambda bb, t: (bb, t, 0)),
                   _tok_spec(LANES, lay), _tok_spec(LANES, lay),
                   _tok_spec(LANES, lay), _full_spec((SUBLANES, LANES))],
        out_shape=[
            jax.ShapeDtypeStruct((b, n, d), F32),
            jax.ShapeDtypeStruct((b, n * ROW_TILE, LANES), F32),
            jax.ShapeDtypeStruct((b, n, LANES), I32),
            jax.ShapeDtypeStruct((b, n, LANES), F32),
            jax.ShapeDtypeStruct((b, n, LANES), I32),
            jax.ShapeDtypeStruct((SUBLANES, LANES), F32),
        ],
        scratch_shapes=[pltpu.VMEM((SUBLANES, LANES), F32)],
        compiler_params=_cparams(("arbitrary", "arbitrary")),
        name="readout_" + mode,
    )(*head_in, w_out.astype(BF16), h, mod_l, norm2_g.reshape(1, d), wr_hi, wr_lo, br)


def _dest_kernel(e_ref, rank_ref, start_ref, dest_ref):
    pos = (rank_ref[0] + start_ref[...]).astype(F32)
    e = e_ref[0]
    lane = lax.broadcasted_iota(I32, e.shape, 1)
    dest = jnp.zeros(e.shape, F32)
    for jx in range(TOP_K):
        slot = jnp.sum(jnp.where(lane == e[:, jx:jx + 1], pos, 0.0), axis=-1, keepdims=True)
        dest = jnp.where(lane == jx, slot, dest)
    dest_ref[0] = dest.astype(I32)


def _dest(top_e, rank, pad_start, lay):
    b, n = lay.batch, lay.n
    return pl.pallas_call(
        _dest_kernel,
        grid=(b, lay.tiles),
        in_specs=[_tok_spec(LANES, lay), _tok_spec(LANES, lay), _full_spec((1, LANES))],
        out_specs=_tok_spec(LANES, lay),
        out_shape=jax.ShapeDtypeStruct((b, n, LANES), I32),
        compiler_params=_cparams(("parallel", "parallel")),
        name="moe_dest",
    )(top_e, rank, pad_start)


def _row_copies_wait(src_block, dst_rows, sem):
    for _ in range(TOP_K):
        pltpu.make_async_copy(src_block, dst_rows, sem).wait()


def _zero_fill(zs_ref, nu_ref, xb_ref, zbuf_ref, sem):
    zbuf_ref[...] = jnp.zeros_like(zbuf_ref)
    blk_rows = zbuf_ref.shape[0]
    n_blocks = xb_ref.shape[0] // blk_rows
    for e in range(N_EXPERTS):
        start = pl.multiple_of(zs_ref[e] * ROW_TILE, ROW_TILE)
        pltpu.make_async_copy(zbuf_ref, xb_ref.at[pl.ds(start, blk_rows), :], sem).start()
    for e in range(N_EXPERTS):
        pltpu.make_async_copy(zbuf_ref, xb_ref.at[pl.ds(0, blk_rows), :], sem).wait()
    for i in range(n_blocks - N_EXPERTS, n_blocks):

        @pl.when(i >= nu_ref[0])
        def _():
            cp = pltpu.make_async_copy(zbuf_ref, xb_ref.at[pl.ds(i * blk_rows, blk_rows), :], sem)
            cp.start()
            cp.wait()


def _dispatch_kernel(dest_ref, zs_ref, nu_ref, v_ref, xb_ref, zbuf_ref, sem):
    rows = v_ref.shape[1] // ROW_TILE

    @pl.when((pl.program_id(0) == 0) & (pl.program_id(1) == 0))
    def _():
        _zero_fill(zs_ref, nu_ref, xb_ref, zbuf_ref, sem)

    def body(r, carry):
        src = v_ref.at[0, pl.ds(pl.multiple_of(r * ROW_TILE, ROW_TILE), ROW_TILE), :]
        for jx in range(TOP_K):
            slot = dest_ref[0, 0, r * TOP_K + jx]
            dst = xb_ref.at[pl.ds(pl.multiple_of(slot * ROW_TILE, ROW_TILE), ROW_TILE), :]
            pltpu.make_async_copy(src, dst, sem).start(priority=jx % DMA_THREADS)
        return carry

    lax.fori_loop(0, rows, body, 0, unroll=DMA_UNROLL)
    _row_copies_wait(v_ref.at[0], xb_ref.at[pl.ds(0, rows * ROW_TILE), :], sem)


def _dispatch(dest_c, zero_start, n_used, v_rows, cap, lay):
    tiles = lay.tiles
    return pl.pallas_call(
        _dispatch_kernel,
        grid=(lay.batch, tiles),
        in_specs=[
            pl.BlockSpec((1, 1, TOKEN_TILE * TOP_K), lambda bb, t: (bb * tiles + t, 0, 0),
                         memory_space=pltpu.SMEM),
            pl.BlockSpec(memory_space=pltpu.SMEM),
            pl.BlockSpec(memory_space=pltpu.SMEM),
            pl.BlockSpec((1, TOKEN_TILE * ROW_TILE, LANES), lambda bb, t: (bb, t, 0)),
        ],
        out_specs=pl.BlockSpec(memory_space=pl.ANY),
        out_shape=jax.ShapeDtypeStruct((cap * ROW_TILE, LANES), F32),
        scratch_shapes=[pltpu.VMEM((MOE_BLOCK * ROW_TILE, LANES), F32), pltpu.SemaphoreType.DMA(())],
        compiler_params=_cparams(("arbitrary", "arbitrary")),
        name="moe_dispatch",
    )(dest_c, zero_start, n_used, v_rows)


def _expert_kernel(be_ref, nx_ref, nu_ref, x_ref, wgu_hbm, bgu_ref, wd_hbm, bd_ref, y_ref,
                   wgu_f32_ref, wd_f32_ref, wgu_bf_ref, wd_bf_ref, slot_ref, sems, *, layer):
    i = pl.program_id(0)
    e = be_ref[i]
    used = i < nu_ref[0]
    new_expert = (i == 0) | (e != be_ref[jnp.maximum(i - 1, 0)])

    def weight_copies(expert, slot):
        return (pltpu.make_async_copy(wgu_hbm.at[layer, expert], wgu_f32_ref.at[slot], sems.at[0, slot]),
                pltpu.make_async_copy(wd_hbm.at[layer, expert], wd_f32_ref.at[slot], sems.at[1, slot]))

    @pl.when(i == 0)
    def _():
        slot_ref[0] = 0
        for cp in weight_copies(e, 0):
            cp.start()

    @pl.when(new_expert)
    def _():
        slot = slot_ref[0]
        for cp in weight_copies(e, slot):
            cp.wait()
        nxt = nx_ref[e]

        @pl.when(nxt != e)
        def _():
            for cp in weight_copies(nxt, 1 - slot):
                cp.start()

        wgu_bf_ref[...] = wgu_f32_ref[slot].astype(BF16)
        wd_bf_ref[...] = wd_f32_ref[slot].astype(BF16)
        slot_ref[0] = 1 - slot

    @pl.when(jnp.logical_not(used))
    def _():
        y_ref[...] = jnp.zeros_like(y_ref)

    @pl.when(used)
    def _():
        x = jnp.concatenate([x_ref[pl.ds(c, MOE_BLOCK, stride=ROW_TILE), :] for c in range(ROW_TILE)], axis=1)
        gu = _dot(x.astype(BF16), wgu_bf_ref[...]) + bgu_ref[0, 0]
        g = jnp.minimum(gu[:, :D_EXPERT], SWIGLU_LIMIT)
        lin = jnp.clip(gu[:, D_EXPERT:], -SWIGLU_LIMIT, SWIGLU_LIMIT)
        hdn = g * _sigmoid(SWIGLU_ALPHA * g) * (lin + 1.0)
        y = _dot(hdn.astype(BF16), wd_bf_ref[...]) + bd_ref[0, 0]
        for c in range(ROW_TILE):
            y_ref[pl.ds(c, MOE_BLOCK, stride=ROW_TILE), :] = y[:, c * LANES:(c + 1) * LANES]


def _experts(block_e, next_e, n_used, xb, w_gu, b_gu, w_d, b_d, layer):
    d, de = D_MODEL, D_EXPERT
    depth = w_gu.shape[0]
    blk_rows = MOE_BLOCK * ROW_TILE
    n_blocks = xb.shape[0] // blk_rows

    def row_map(i, be, nx, nu):
        return (jnp.minimum(i, nu[0] - 1), 0)

    def bias_map(i, be, nx, nu):
        return (layer, be[i], 0, 0)

    grid_spec = pltpu.PrefetchScalarGridSpec(
        num_scalar_prefetch=3,
        grid=(n_blocks,),
        in_specs=[
            pl.BlockSpec((blk_rows, LANES), row_map),
            pl.BlockSpec(memory_space=pl.ANY),
            pl.BlockSpec((1, 1, 1, 2 * de), bias_map),
            pl.BlockSpec(memory_space=pl.ANY),
            pl.BlockSpec((1, 1, 1, d), bias_map),
        ],
        out_specs=pl.BlockSpec((blk_rows, LANES), lambda i, be, nx, nu: (i, 0)),
        scratch_shapes=[
            pltpu.VMEM((2, d, 2 * de), F32), pltpu.VMEM((2, de, d), F32),
            pltpu.VMEM((d, 2 * de), BF16), pltpu.VMEM((de, d), BF16),
            pltpu.SMEM((1,), I32), pltpu.SemaphoreType.DMA((2, 2)),
        ],
    )
    return pl.pallas_call(
        functools.partial(_expert_kernel, layer=layer),
        grid_spec=grid_spec,
        out_shape=jax.ShapeDtypeStruct(xb.shape, F32),
        compiler_params=_cparams(("arbitrary",), vmem=EXPERT_VMEM_LIMIT),
        name="moe_experts",
    )(block_e, next_e, n_used, xb, w_gu, b_gu.reshape(depth, N_EXPERTS, 1, 2 * de), w_d,
      b_d.reshape(depth, N_EXPERTS, 1, d))


def _combine_kernel(dest_ref, gate_ref, yb_ref, h_ref, mod_ref, out_ref, ybuf_ref, acc_ref, sem):
    rows = h_ref.shape[1]

    def issue(r, carry):
        base = pl.multiple_of(r * ROW_TILE, ROW_TILE)
        for jx in range(TOP_K):
            slot = dest_ref[0, 0, r * TOP_K + jx]
            src = yb_ref.at[pl.ds(pl.multiple_of(slot * ROW_TILE, ROW_TILE), ROW_TILE), :]
            pltpu.make_async_copy(src, ybuf_ref.at[jx, pl.ds(base, ROW_TILE), :], sem).start(
                priority=jx % DMA_THREADS)
        return carry

    lax.fori_loop(0, rows, issue, 0, unroll=DMA_UNROLL)
    _row_copies_wait(yb_ref.at[pl.ds(0, rows * ROW_TILE), :], ybuf_ref.at[0], sem)

    def accumulate(r, carry):
        base = pl.multiple_of(r * ROW_TILE, ROW_TILE)
        a = gate_ref[0, 0, r * TOP_K] * ybuf_ref[0, pl.ds(base, ROW_TILE), :]
        for jx in range(1, TOP_K):
            a = a + gate_ref[0, 0, r * TOP_K + jx] * ybuf_ref[jx, pl.ds(base, ROW_TILE), :]
        acc_ref[pl.ds(base, ROW_TILE), :] = a
        return carry

    lax.fori_loop(0, rows, accumulate, 0, unroll=DMA_UNROLL)
    for c in range(ROW_TILE):
        cs = slice(c * LANES, (c + 1) * LANES)
        out_ref[0, :, cs] = h_ref[0, :, cs] + mod_ref[0, 5:6, cs] * acc_ref[pl.ds(c, rows, stride=ROW_TILE), :]


def _combine(dest_c, gate_c, yb, h, mod_l, lay):
    d = D_MODEL
    tiles = lay.tiles
    smem_spec = pl.BlockSpec((1, 1, TOKEN_TILE * TOP_K), lambda bb, t: (bb * tiles + t, 0, 0),
                             memory_space=pltpu.SMEM)
    return pl.pallas_call(
        _combine_kernel,
        grid=(lay.batch, tiles),
        in_specs=[smem_spec, smem_spec, pl.BlockSpec(memory_space=pl.ANY), _tok_spec(d, lay), _mod_spec(lay)],
        out_specs=_tok_spec(d, lay),
        out_shape=jax.ShapeDtypeStruct((lay.batch, lay.n, d), F32),
        scratch_shapes=[pltpu.VMEM((TOP_K, TOKEN_TILE * ROW_TILE, LANES), F32),
                        pltpu.VMEM((TOKEN_TILE * ROW_TILE, LANES), F32), pltpu.SemaphoreType.DMA(())],
        compiler_params=_cparams(("arbitrary", "arbitrary")),
        name="moe_combine",
    )(dest_c, gate_c, yb, h, mod_l)


def _moe(h, v_rows, top_e, gate, rank, counts, mod_l, w_gu, b_gu, w_d, b_d, layer, lay):
    n_tokens = lay.batch * lay.n
    n_assign = n_tokens * TOP_K
    cap = (-(-n_assign // MOE_BLOCK) + N_EXPERTS) * MOE_BLOCK
    n_blocks = cap // MOE_BLOCK
    cnt = counts[0, :N_EXPERTS].astype(I32)
    padded = (cnt + MOE_BLOCK - 1) // MOE_BLOCK * MOE_BLOCK
    pad_ends = jnp.cumsum(padded)
    pad_starts = pad_ends - padded
    n_used = (pad_ends[-1] // MOE_BLOCK).astype(I32).reshape(1)
    blk = jnp.minimum(jnp.arange(n_blocks, dtype=I32), n_used[0] - 1) * MOE_BLOCK
    block_e = jnp.minimum(jnp.sum((pad_ends[None, :] <= blk[:, None]).astype(I32), axis=1), N_EXPERTS - 1)
    start_row = jnp.zeros((1, LANES), I32).at[0, :N_EXPERTS].set(pad_starts)
    zero_start = jnp.minimum(pad_starts + cnt, cap - MOE_BLOCK).astype(I32)

    dest = _dest(top_e, rank, start_row, lay)
    n_tiles = lay.batch * lay.tiles
    dest_c = dest[:, :, :TOP_K].reshape(n_tiles, 1, TOKEN_TILE * TOP_K)
    gate_c = gate[:, :, :TOP_K].reshape(n_tiles, 1, TOKEN_TILE * TOP_K)
    ids = jnp.arange(N_EXPERTS, dtype=I32)
    later = (ids[None, :] > ids[:, None]) & (padded[None, :] > 0)
    next_e = jnp.where(jnp.any(later, axis=1), jnp.argmax(later, axis=1).astype(I32), ids)
    xb = _dispatch(dest_c, zero_start, n_used, v_rows, cap, lay)
    yb = _experts(block_e, next_e, n_used, xb, w_gu, b_gu, w_d, b_d, layer)
    return _combine(dest_c, gate_c, yb, h, mod_l, lay)


def kernel(x, c, ctx, c_ctx, norm1_g, norm2_g, mod_w, mod_b,
           mlstm_w_in, mlstm_conv_w, mlstm_gate_b, mlstm_out_norm_g, mlstm_w_out,
           attn_w_in, attn_q_norm_g, attn_k_norm_g, attn_w_out,
           gla_w_in, gla_alpha_w2, gla_alpha_b, gla_out_norm_g, gla_w_out,
           router_w, router_b, moe_w_gu, moe_b_gu, moe_w_down, moe_b_down):
    batch, n_lat, d = x.shape
    n_ctx = ctx.shape[1]
    depth = norm1_g.shape[0]
    assert d == D_MODEL
    lay = _Layout(batch, n_ctx, n_lat)

    cond_rows = -(-(batch + 1) // SUBLANES) * SUBLANES
    cond = jnp.zeros((cond_rows, d), F32).at[:batch].set(c).at[batch].set(c_ctx)
    mod_all = _modulation(cond, mod_w, mod_b)

    h = jnp.concatenate([ctx, x], axis=1)
    for layer in range(depth):
        kind, j = layer % 3, layer // 3
        mod_l = mod_all[layer]
        if kind == 0:
            qk_pre, v_t, o_gate, gcol, grow = _mlstm_in(h, norm1_g[layer], mod_l, mlstm_w_in[j],
                                                        mlstm_gate_b[j], lay)
            q_t, k_m = _mlstm_conv(qk_pre, mlstm_conv_w[j], lay)
            h_dir = _mlstm_scan(q_t, k_m, v_t, gcol, grow, lay)
            mixer = ("mlstm", (h_dir, o_gate, mlstm_out_norm_g[j]), mlstm_w_out[j])
        elif kind == 1:
            q_a, k_a, v_a = _gqa_in(h, norm1_g[layer], mod_l, attn_w_in[j], attn_q_norm_g[j],
                                    attn_k_norm_g[j], lay)
            mixer = ("attn", (_attention(q_a, k_a, v_a, lay),), attn_w_out[j])
        else:
            q_g, k_g, v_g, g_gate, la_f, la_b = _gla_in(h, norm1_g[layer], mod_l, gla_w_in[j],
                                                        gla_alpha_w2[j], gla_alpha_b[j], lay)
            o_f = _gla_scan(q_g, k_g, v_g, la_f, lay, reverse=False)
            o_b = _gla_scan(q_g, k_g, v_g, la_b, lay, reverse=True)
            mixer = ("gla", (o_f, o_b, g_gate, gla_out_norm_g[j]), gla_w_out[j])
        mode, mixer_outs, w_out = mixer
        h, v_moe, top_e, gate, rank, counts = _readout(mode, mixer_outs, w_out, h, mod_l, norm2_g[layer],
                                                       router_w[layer], router_b[layer], lay)
        h = _moe(h, v_moe, top_e, gate, rank, counts, mod_l,
                 moe_w_gu, moe_b_gu, moe_w_down, moe_b_down, layer, lay)
    return h[:, n_ctx:]
```

```python
import functools

import jax
import jax.numpy as jnp
from jax import lax
from jax.experimental import pallas as pl
from jax.experimental.pallas import tpu as pltpu

F32 = jnp.float32
BF16 = jnp.bfloat16
I32 = jnp.int32
HIGHEST = lax.Precision.HIGHEST

D_MODEL = 1024
GRID_W = 64
NORM_EPS = 1e-6

MLSTM_HEADS = 8
MLSTM_DQK = 64
MLSTM_DV = 128
MLSTM_QK = MLSTM_HEADS * MLSTM_DQK

ATTN_HEADS = 8
ATTN_KV_HEADS = 4
ATTN_GROUP = ATTN_HEADS // ATTN_KV_HEADS
ATTN_HEAD_DIM = 128
ROPE_THETA = 10000.0

GLA_HEADS = 4
GLA_DK = 128
GLA_DV = 256
GLA_GATE_RANK = 16
GLA_TAU = 16.0
GLA_QK = GLA_HEADS * GLA_DK

N_EXPERTS = 32
TOP_K = 4
D_EXPERT = D_MODEL
SWIGLU_LIMIT = 7.0
SWIGLU_ALPHA = 1.702

LANES = 128
SUBLANES = 8
TOKEN_TILE = 256
MLSTM_CHUNK = 256
GLA_TILE = 128
GLA_CHUNK = 32
ATTN_KV_CHUNK = 512
ATTN_Q_SCALE = ATTN_HEAD_DIM ** -0.5 * 1.4426950408889634
MOE_BLOCK = 512
ROW_TILE = D_MODEL // LANES
assert ROW_TILE == SUBLANES
DMA_UNROLL = 8
DMA_THREADS = 2
VMEM_LIMIT = 48 * 1024 * 1024
EXPERT_VMEM_LIMIT = 56 * 1024 * 1024
NEG_BIG = -1e30


def _cparams(semantics, vmem=VMEM_LIMIT):
    return pltpu.CompilerParams(dimension_semantics=semantics, vmem_limit_bytes=vmem)


def _log_sigmoid(x):
    return jnp.minimum(x, 0.0) - jnp.log(1.0 + jnp.exp(-jnp.abs(x)))


def _sigmoid(x):
    return 1.0 / (1.0 + jnp.exp(-x))


def _norm_mod(x, g, shift, scale):
    ms = jnp.mean(x * x, axis=-1, keepdims=True)
    y = x * lax.rsqrt(ms + NORM_EPS) * g
    return y * (1.0 + scale) + shift


def _head_rms(x, head_dim):
    outs = []
    for i in range(x.shape[1] // head_dim):
        xs = x[:, i * head_dim:(i + 1) * head_dim]
        ms = jnp.mean(xs * xs, axis=-1, keepdims=True)
        outs.append(xs * lax.rsqrt(ms + NORM_EPS))
    return jnp.concatenate(outs, axis=1)


def _dot(a, b):
    return jnp.dot(a, b, preferred_element_type=F32)


def _dot_nt(a, b):
    return lax.dot_general(a, b, (((1,), (1,)), ((), ())), preferred_element_type=F32)


def _dot_tn(a, b):
    return lax.dot_general(a, b, (((0,), (0,)), ((), ())), preferred_element_type=F32)


def _mod_kernel(c_ref, w_ref, b_ref, o_ref):
    x = c_ref[...]
    x = x * _sigmoid(x)
    o_ref[0, 0] = _dot(x.astype(BF16), w_ref[0].astype(BF16)) + b_ref[0, 0]


def _modulation(cond_rows, mod_w, mod_b):
    depth, d, _ = mod_w.shape
    rows = cond_rows.shape[0]
    out = pl.pallas_call(
        _mod_kernel,
        grid=(depth, 6),
        in_specs=[
            pl.BlockSpec((rows, d), lambda l, j: (0, 0)),
            pl.BlockSpec((1, d, d), lambda l, j: (l, 0, j)),
            pl.BlockSpec((1, 1, 1, d), lambda l, j: (l, j, 0, 0)),
        ],
        out_specs=pl.BlockSpec((1, 1, rows, d), lambda l, j: (l, j, 0, 0)),
        out_shape=jax.ShapeDtypeStruct((depth, 6, rows, d), F32),
        compiler_params=_cparams(("parallel", "parallel")),
        name="modulation",
    )(cond_rows, mod_w, mod_b.reshape(depth, 6, 1, d))
    return out.transpose(0, 2, 1, 3)


class _Layout:
    def __init__(self, batch, n_ctx, n_lat):
        self.batch = batch
        self.n_ctx = n_ctx
        self.n_lat = n_lat
        self.n = n_ctx + n_lat
        assert n_ctx % TOKEN_TILE == 0 and n_lat % TOKEN_TILE == 0
        assert n_ctx % MLSTM_CHUNK == 0 and n_lat % MLSTM_CHUNK == 0
        assert n_ctx % GLA_TILE == 0 and n_lat % GLA_TILE == 0
        self.tiles = self.n // TOKEN_TILE
        self.ctx_tiles = n_ctx // TOKEN_TILE
        self.ctx_row = batch

    def mod_index(self, b, t):
        return jnp.where(t < self.ctx_tiles, self.ctx_row, b)


def _tok_spec(width, lay):
    return pl.BlockSpec((1, TOKEN_TILE, width), lambda b, t: (b, t, 0))


def _feat_spec(features):
    return pl.BlockSpec((1, features, TOKEN_TILE), lambda b, t: (b, 0, t))


def _full_spec(shape):
    nd = len(shape)
    return pl.BlockSpec(shape, lambda b, t: (0,) * nd)


def _mod_spec(lay):
    return pl.BlockSpec((1, 6, D_MODEL), lambda b, t: (lay.mod_index(b, t), 0, 0))


def _mlstm_in_kernel(h_ref, g_ref, mod_ref, w_ref, wvot_ref, wg_ref, wgt_ref, gb_ref, gbt_ref,
                     qk_ref, vt_ref, ot_ref, gc_ref, gr_ref):
    mod = mod_ref[0]
    u = _norm_mod(h_ref[0], g_ref[...], mod[0:1], mod[1:2]).astype(BF16)
    qk_ref[0] = _dot(u, w_ref[...])
    vo_t = _dot_nt(wvot_ref[...], u)
    vt_ref[0] = vo_t[:D_MODEL].astype(BF16)
    ot_ref[0] = vo_t[D_MODEL:]
    gc = _dot(u, wg_ref[...]) + gb_ref[...]
    lane = lax.broadcasted_iota(I32, gc.shape, 1)
    gc_ref[0] = jnp.where(((lane >> 3) & 1) == 1, _log_sigmoid(gc), gc)
    gr = _dot_nt(wgt_ref[...], u) + gbt_ref[...]
    sub = lax.broadcasted_iota(I32, gr.shape, 0)
    gr_ref[0] = jnp.where(((sub >> 3) & 1) == 1, _log_sigmoid(gr), gr)


def _mlstm_in(h, norm_g, mod_l, w_in, gate_b, lay):
    d = D_MODEL
    n_qk = 2 * MLSTM_QK
    n_main = n_qk + 2 * d
    n_gate = 4 * MLSTM_HEADS
    w_qk = w_in[:, :n_qk].astype(BF16)
    w_vo_t = w_in[:, n_qk:n_main].T.astype(BF16)
    w_gate = w_in[:, n_main:]
    wg = jnp.zeros((d, LANES), F32).at[:, :n_gate].set(w_gate).astype(BF16)
    wgt = w_gate.T.astype(BF16)
    gb = jnp.zeros((1, LANES), F32).at[0, :n_gate].set(gate_b.reshape(-1))
    gbt = gate_b.reshape(n_gate, 1)
    b, n = lay.batch, lay.n
    return pl.pallas_call(
        _mlstm_in_kernel,
        grid=(b, lay.tiles),
        in_specs=[
            _tok_spec(d, lay), _full_spec((1, d)), _mod_spec(lay),
            _full_spec((d, n_qk)), _full_spec((2 * d, d)), _full_spec((d, LANES)), _full_spec((n_gate, d)),
            _full_spec((1, LANES)), _full_spec((n_gate, 1)),
        ],
        out_specs=[
            _tok_spec(n_qk, lay), _feat_spec(d), _feat_spec(d), _tok_spec(LANES, lay), _feat_spec(n_gate),
        ],
        out_shape=[
            jax.ShapeDtypeStruct((b, n, n_qk), F32),
            jax.ShapeDtypeStruct((b, d, n), BF16),
            jax.ShapeDtypeStruct((b, d, n), F32),
            jax.ShapeDtypeStruct((b, n, LANES), F32),
            jax.ShapeDtypeStruct((b, n_gate, n), F32),
        ],
        compiler_params=_cparams(("parallel", "parallel")),
        name="mlstm_in",
    )(h, norm_g.reshape(1, d), mod_l, w_qk, w_vo_t, wg, wgt, gb, gbt)


def _mlstm_conv_kernel(x_ref, xp_ref, xn_ref, cw_ref, qt_ref, k_ref, *, tiles, ctx_tiles):
    t = pl.program_id(1)
    x = x_ref[0]
    rows = x.shape[0]
    seg_first = (t == 0) | (t == ctx_tiles)
    seg_last = (t == ctx_tiles - 1) | (t == tiles - 1)
    prev_row = jnp.where(seg_first, 0.0, xp_ref[0, SUBLANES - 1:SUBLANES, :])
    next_row = jnp.where(seg_last, 0.0, xn_ref[0, 0:1, :])
    rid = lax.broadcasted_iota(I32, x.shape, 0)
    x_m1 = jnp.where(rid == 0, prev_row, pltpu.roll(x, 1, 0))
    x_p1 = jnp.where(rid == rows - 1, next_row, pltpu.roll(x, rows - 1, 0))
    cw = cw_ref[...]
    y = cw[0:1] * x_m1 + cw[1:2] * x + cw[2:3] * x_p1
    qk = y * _sigmoid(y)
    qt_ref[0] = (qk[:, :MLSTM_QK] * MLSTM_DQK ** -0.5).T.astype(BF16)
    k_ref[0] = qk[:, MLSTM_QK:].astype(BF16)


def _mlstm_conv(qk_pre, conv_w, lay):
    b, n = lay.batch, lay.n
    n_qk = 2 * MLSTM_QK
    halo = TOKEN_TILE // SUBLANES
    n_halo = n // SUBLANES
    kern = functools.partial(_mlstm_conv_kernel, tiles=lay.tiles, ctx_tiles=lay.ctx_tiles)
    return pl.pallas_call(
        kern,
        grid=(b, lay.tiles),
        in_specs=[
            _tok_spec(n_qk, lay),
            pl.BlockSpec((1, SUBLANES, n_qk), lambda bb, t: (bb, jnp.maximum(t * halo - 1, 0), 0)),
            pl.BlockSpec((1, SUBLANES, n_qk), lambda bb, t: (bb, jnp.minimum((t + 1) * halo, n_halo - 1), 0)),
            _full_spec((3, n_qk)),
        ],
        out_specs=[_feat_spec(MLSTM_QK), _tok_spec(MLSTM_QK, lay)],
        out_shape=[
            jax.ShapeDtypeStruct((b, MLSTM_QK, n), BF16),
            jax.ShapeDtypeStruct((b, n, MLSTM_QK), BF16),
        ],
        compiler_params=_cparams(("parallel", "parallel")),
        name="mlstm_conv",
    )(qk_pre, qk_pre, qk_pre, conv_w)


def _scan_chunk_index(reverse, j, n_chunks, ctx_chunks):
    back = jnp.where(j < ctx_chunks, ctx_chunks - 1 - j, n_chunks - 1 - (j - ctx_chunks))
    return jnp.where(reverse, back, j)


def _mlstm_scan_kernel(qt_ref, k_ref, vt_ref, gc_ref, gr_ref, out_ref, st_ref, m_ref):
    L = MLSTM_CHUNK
    dk, dv, nh = MLSTM_DQK, MLSTM_DV, MLSTM_HEADS
    d = pl.program_id(1)
    j = pl.program_id(2)

    @pl.when(j == 0)
    def _():
        st_ref[...] = jnp.zeros_like(st_ref)
        m_ref[...] = jnp.zeros_like(m_ref)

    q_t = qt_ref[0]
    k = k_ref[0]
    v_t = vt_ref[0]

    gcol = gc_ref[0]
    gcol = jnp.where(d == 0, gcol, pltpu.roll(gcol, LANES - 2 * nh, 1))
    grow = gr_ref[0, pl.ds(pl.multiple_of(d * 2 * nh, 2 * nh), 2 * nh), :]

    row = lax.broadcasted_iota(I32, (L, L), 0)
    col = lax.broadcasted_iota(I32, (L, L), 1)
    sgn = 1 - 2 * d
    mask_t = sgn * (col - row) >= 0
    cum_rows = jnp.dot(grow, mask_t.astype(F32), precision=HIGHEST, preferred_element_type=F32)
    cum_cols = jnp.dot((sgn * (row - col) >= 0).astype(F32), gcol, precision=HIGHEST,
                       preferred_element_type=F32)
    b_minus_i = cum_cols - pltpu.roll(gcol, nh, 1)
    b_tot = jnp.sum(grow[nh:], axis=-1, keepdims=True)

    ones_row = (lax.broadcasted_iota(I32, (LANES, L), 0) == 0).astype(BF16)
    for h in range(nh):
        b_t = cum_rows[nh + h:nh + h + 1, :]
        li = grow[h:h + 1, :]
        m_prev = m_ref[h, 0:1, 0:1]
        dmat = jnp.where(mask_t, b_t - b_minus_i[:, nh + h:nh + h + 1], -jnp.inf)
        inter = b_t + m_prev
        m_t = jnp.maximum(inter, jnp.max(dmat, axis=0, keepdims=True))
        w = jnp.exp(dmat - m_t)
        a_inter = jnp.exp(inter - m_t)
        qt_h = q_t[h * dk:(h + 1) * dk, :]
        k_h = k[:, h * dk:(h + 1) * dk]
        s = (_dot(k_h, qt_h) * w).astype(BF16)
        vt_aug = jnp.concatenate([v_t[h * dv:(h + 1) * dv, :], ones_row], axis=0)
        st = st_ref[h]
        r = a_inter * _dot(st.astype(BF16), qt_h) + _dot(vt_aug, s)
        den = r[dv:dv + 1, :]
        out_ref[0, 0, h * dv:(h + 1) * dv, :] = r[:dv] / jnp.maximum(jnp.abs(den), jnp.exp(-m_t))
        b_last = b_tot[h:h + 1, :]
        g = b_last - b_t + li
        m_new = jnp.maximum(b_last + m_prev, jnp.max(g, axis=-1, keepdims=True))
        decay = jnp.exp(b_last + m_prev - m_new)
        wg = jnp.exp(g - m_new)
        st_ref[h] = decay * st + _dot((vt_aug.astype(F32) * wg).astype(BF16), k_h)
        m_ref[h] = jnp.broadcast_to(m_new, (SUBLANES, LANES))


def _mlstm_scan(q_t, k, v_t, gcol, grow, lay):
    b, n = lay.batch, lay.n
    L = MLSTM_CHUNK
    n_chunks, ctx_chunks = n // L, lay.n_ctx // L

    def cidx(d, j):
        return _scan_chunk_index(d == 1, j, n_chunks, ctx_chunks)

    return pl.pallas_call(
        _mlstm_scan_kernel,
        grid=(b, 2, n_chunks),
        in_specs=[
            pl.BlockSpec((1, MLSTM_QK, L), lambda bb, d, j: (bb, 0, cidx(d, j))),
            pl.BlockSpec((1, L, MLSTM_QK), lambda bb, d, j: (bb, cidx(d, j), 0)),
            pl.BlockSpec((1, D_MODEL, L), lambda bb, d, j: (bb, 0, cidx(d, j))),
            pl.BlockSpec((1, L, LANES), lambda bb, d, j: (bb, cidx(d, j), 0)),
            pl.BlockSpec((1, 4 * MLSTM_HEADS, L), lambda bb, d, j: (bb, 0, cidx(d, j))),
        ],
        out_specs=pl.BlockSpec((1, 1, D_MODEL, L), lambda bb, d, j: (d, bb, 0, cidx(d, j))),
        out_shape=jax.ShapeDtypeStruct((2, b, D_MODEL, n), F32),
        scratch_shapes=[
            pltpu.VMEM((MLSTM_HEADS, MLSTM_DV + LANES, MLSTM_DQK), F32),
            pltpu.VMEM((MLSTM_HEADS, SUBLANES, LANES), F32),
        ],
        compiler_params=_cparams(("parallel", "parallel", "arbitrary")),
        name="mlstm_scan",
    )(q_t, k, v_t, gcol, grow)


def _gqa_in_kernel(h_ref, g_ref, mod_ref, w_ref, qg_ref, kg_ref, cos_ref, sin_ref, q_ref, k_ref, v_ref):
    hd = ATTN_HEAD_DIM
    mod = mod_ref[0]
    u = _norm_mod(h_ref[0], g_ref[...], mod[0:1], mod[1:2]).astype(BF16)
    p = _dot(u, w_ref[...])
    cos = cos_ref[...]
    sin = sin_ref[...]
    lane = lax.broadcasted_iota(I32, cos.shape, 1)
    first_half = (lane & (hd // 4)) == 0

    def rope(xh):
        swapped = jnp.where(first_half, pltpu.roll(xh, hd - hd // 4, 1), pltpu.roll(xh, hd // 4, 1))
        return xh * cos + swapped * sin

    nq, nk = ATTN_HEADS * hd, ATTN_KV_HEADS * hd
    qn = _head_rms(p[:, :nq], hd)
    kn = _head_rms(p[:, nq:nq + nk], hd)
    qg = qg_ref[...]
    kg = kg_ref[...]
    for i in range(ATTN_HEADS):
        q_ref[0, :, i * hd:(i + 1) * hd] = (rope(qn[:, i * hd:(i + 1) * hd] * qg) * ATTN_Q_SCALE).astype(BF16)
    for i in range(ATTN_KV_HEADS):
        k_ref[0, :, i * hd:(i + 1) * hd] = rope(kn[:, i * hd:(i + 1) * hd] * kg).astype(BF16)
    v_ref[0] = p[:, nq + nk:].astype(BF16)


def _rope_tables(lay):
    hd = ATTN_HEAD_DIM
    quarter = hd // 4
    inv = ROPE_THETA ** (-jnp.arange(quarter, dtype=F32) / quarter)
    pos = jnp.arange(lay.n_lat)
    rows = (pos // GRID_W).astype(F32)
    cols = (pos % GRID_W).astype(F32)
    ang = jnp.concatenate([jnp.tile(rows[:, None] * inv, (1, 2)), jnp.tile(cols[:, None] * inv, (1, 2))], axis=1)
    sign = jnp.tile(jnp.concatenate([-jnp.ones(quarter, F32), jnp.ones(quarter, F32)]), 2)
    cos = jnp.concatenate([jnp.ones((lay.n_ctx, hd), F32), jnp.cos(ang)], axis=0)
    sin = jnp.concatenate([jnp.zeros((lay.n_ctx, hd), F32), jnp.sin(ang) * sign], axis=0)
    return cos, sin


def _gqa_in(h, norm_g, mod_l, w_in, q_g, k_g, lay):
    d, hd = D_MODEL, ATTN_HEAD_DIM
    nq, nk = ATTN_HEADS * hd, ATTN_KV_HEADS * hd
    cos, sin = _rope_tables(lay)
    b, n = lay.batch, lay.n
    tab_spec = pl.BlockSpec((TOKEN_TILE, hd), lambda bb, t: (t, 0))
    return pl.pallas_call(
        _gqa_in_kernel,
        grid=(b, lay.tiles),
        in_specs=[
            _tok_spec(d, lay), _full_spec((1, d)), _mod_spec(lay), _full_spec((d, nq + 2 * nk)),
            _full_spec((1, hd)), _full_spec((1, hd)), tab_spec, tab_spec,
        ],
        out_specs=[_tok_spec(nq, lay), _tok_spec(nk, lay), _tok_spec(nk, lay)],
        out_shape=[
            jax.ShapeDtypeStruct((b, n, nq), BF16),
            jax.ShapeDtypeStruct((b, n, nk), BF16),
            jax.ShapeDtypeStruct((b, n, nk), BF16),
        ],
        compiler_params=_cparams(("parallel", "parallel")),
        name="gqa_in",
    )(h, norm_g.reshape(1, d), mod_l, w_in.astype(BF16), q_g.reshape(1, hd), k_g.reshape(1, hd), cos, sin)


def _attn_kernel(q_ref, k_ref, v_ref, o_ref, *, ctx_tiles, n_ctx, n_all):
    hd = ATTN_HEAD_DIM
    t = pl.program_id(2)
    q = q_ref[0]
    rows = q.shape[0]
    q2 = jnp.concatenate([q[:, g * hd:(g + 1) * hd] for g in range(ATTN_GROUP)], axis=0)

    def attend(n_keys):
        bounds = [0, n_ctx] + list(range(n_ctx + ATTN_KV_CHUNK, n_keys + 1, ATTN_KV_CHUNK))
        assert bounds[-1] == n_keys
        m = l = acc = None
        for lo, hi in zip(bounds[:-1], bounds[1:]):
            s = _dot_nt(q2, k_ref[0, lo:hi, :])
            mc = jnp.max(s, axis=-1, keepdims=True)
            if m is None:
                m_new = mc
                p = jnp.exp2(s - m_new)
                l = jnp.sum(p, axis=-1, keepdims=True)
                acc = _dot(p.astype(BF16), v_ref[0, lo:hi, :])
            else:
                m_new = jnp.maximum(m, mc)
                alpha = jnp.exp2(m - m_new)
                p = jnp.exp2(s - m_new)
                l = alpha * l + jnp.sum(p, axis=-1, keepdims=True)
                acc = alpha * acc + _dot(p.astype(BF16), v_ref[0, lo:hi, :])
            m = m_new
        o = acc / l
        for g in range(ATTN_GROUP):
            o_ref[0, :, g * hd:(g + 1) * hd] = o[g * rows:(g + 1) * rows].astype(BF16)

    @pl.when(t < ctx_tiles)
    def _():
        attend(n_ctx)

    @pl.when(t >= ctx_tiles)
    def _():
        attend(n_all)


def _attention(q, k, v, lay):
    hd = ATTN_HEAD_DIM
    b, n = lay.batch, lay.n
    gw = ATTN_GROUP * hd
    kern = functools.partial(_attn_kernel, ctx_tiles=lay.ctx_tiles, n_ctx=lay.n_ctx, n_all=n)
    return pl.pallas_call(
        kern,
        grid=(b, ATTN_KV_HEADS, lay.tiles),
        in_specs=[
            pl.BlockSpec((1, TOKEN_TILE, gw), lambda bb, kh, t: (bb, t, kh)),
            pl.BlockSpec((1, n, hd), lambda bb, kh, t: (bb, 0, kh)),
            pl.BlockSpec((1, n, hd), lambda bb, kh, t: (bb, 0, kh)),
        ],
        out_specs=pl.BlockSpec((1, TOKEN_TILE, gw), lambda bb, kh, t: (bb, t, kh)),
        out_shape=jax.ShapeDtypeStruct((b, n, ATTN_HEADS * hd), BF16),
        compiler_params=_cparams(("parallel", "parallel", "parallel")),
        name="attention",
    )(q, k, v)


def _gla_in_kernel(h_ref, g_ref, mod_ref, w_ref, wa_ref, aw_ref, ab_ref,
                   q_ref, k_ref, v_ref, gate_ref, laf_ref, lab_ref):
    d = D_MODEL
    mod = mod_ref[0]
    u = _norm_mod(h_ref[0], g_ref[...], mod[0:1], mod[1:2]).astype(BF16)
    p = _dot(u, w_ref[...])
    q_ref[0] = p[:, :GLA_QK] * GLA_DK ** -0.5
    k_ref[0] = p[:, GLA_QK:2 * GLA_QK]
    v_ref[0] = p[:, 2 * GLA_QK:2 * GLA_QK + d].astype(BF16)
    gate_ref[0] = p[:, 2 * GLA_QK + d:]
    a_low = _dot(u, wa_ref[...]).astype(BF16)
    pre = _dot(a_low, aw_ref[...]) + ab_ref[...]
    la = _log_sigmoid(pre) * (1.0 / GLA_TAU)
    laf_ref[0] = la[:, :GLA_QK]
    lab_ref[0] = la[:, GLA_QK:]


def _gla_in(h, norm_g, mod_l, w_in, alpha_w2, alpha_b, lay):
    d, r = D_MODEL, GLA_GATE_RANK
    n_main = 2 * GLA_QK + 2 * d
    w_main = w_in[:, :n_main].astype(BF16)
    wa = jnp.zeros((d, LANES), F32).at[:, :2 * r].set(w_in[:, n_main:]).astype(BF16)
    aw = jnp.zeros((LANES, 2 * GLA_QK), F32)
    aw = aw.at[:r, :GLA_QK].set(alpha_w2[0]).at[r:2 * r, GLA_QK:].set(alpha_w2[1]).astype(BF16)
    ab = alpha_b.reshape(1, 2 * GLA_QK)
    b, n = lay.batch, lay.n
    return pl.pallas_call(
        _gla_in_kernel,
        grid=(b, lay.tiles),
        in_specs=[
            _tok_spec(d, lay), _full_spec((1, d)), _mod_spec(lay), _full_spec((d, n_main)),
            _full_spec((d, LANES)), _full_spec((LANES, 2 * GLA_QK)), _full_spec((1, 2 * GLA_QK)),
        ],
        out_specs=[_tok_spec(GLA_QK, lay), _tok_spec(GLA_QK, lay), _tok_spec(d, lay), _tok_spec(d, lay),
                   _tok_spec(GLA_QK, lay), _tok_spec(GLA_QK, lay)],
        out_shape=[
            jax.ShapeDtypeStruct((b, n, GLA_QK), F32),
            jax.ShapeDtypeStruct((b, n, GLA_QK), F32),
            jax.ShapeDtypeStruct((b, n, d), BF16),
            jax.ShapeDtypeStruct((b, n, d), F32),
            jax.ShapeDtypeStruct((b, n, GLA_QK), F32),
            jax.ShapeDtypeStruct((b, n, GLA_QK), F32),
        ],
        compiler_params=_cparams(("parallel", "parallel")),
        name="gla_in",
    )(h, norm_g.reshape(1, d), mod_l, w_main, wa, aw, ab)


def _gla_scan_kernel(q_ref, k_ref, v_ref, la_ref, out_ref, st_ref, *, reverse):
    T, C = GLA_TILE, GLA_CHUNK
    dk, dv, nh = GLA_DK, GLA_DV, GLA_HEADS
    j = pl.program_id(1)

    @pl.when(j == 0)
    def _():
        st_ref[...] = jnp.zeros_like(st_ref)

    n_chunks = T // C
    row = lax.broadcasted_iota(I32, (T, T), 0)
    col = lax.broadcasted_iota(I32, (T, T), 1)
    before = (col >= row) if reverse else (col <= row)
    b = jnp.dot(before.astype(F32), la_ref[0], precision=HIGHEST, preferred_element_type=F32)
    b_end = b[0:1] if reverse else b[T - 1:T]
    q = q_ref[0]
    k = k_ref[0]
    q_state = (q * jnp.exp(b)).astype(BF16)
    k_state = (k * jnp.exp(b_end - b)).astype(BF16)
    decay = jnp.exp(b_end)

    def rows_of(c):
        return slice(c * C, (c + 1) * C)

    b_mid = jnp.concatenate(
        [jnp.broadcast_to(b[c * C + C // 2:c * C + C // 2 + 1], (C, b.shape[1])) for c in range(n_chunks)], axis=0)
    q_mid = (q * jnp.exp(b - b_mid)).astype(BF16)
    k_mid = (k * jnp.exp(b_mid - b)).astype(BF16)
    same_chunk = (row // C) == (col // C)
    scan_order = list(reversed(range(n_chunks))) if reverse else list(range(n_chunks))
    later = scan_order[1:]
    q_off, k_off, off_mask = {}, {}, {}
    for c in later:
        edge = (c + 1) * C if reverse else c * C - 1
        b_edge = b[edge:edge + 1]
        q_off[c] = (q[rows_of(c)] * jnp.exp(b[rows_of(c)] - b_edge)).astype(BF16)
        k_off[c] = (k * jnp.exp(jnp.minimum(b_edge - b, 0.0))).astype(BF16)
        ccol = lax.broadcasted_iota(I32, (C, T), 1)
        off_mask[c] = (ccol >= (c + 1) * C) if reverse else (ccol < c * C)

    for h in range(nh):
        ks = slice(h * dk, (h + 1) * dk)
        vs = slice(h * dv, (h + 1) * dv)
        v_h = v_ref[0, :, vs]
        a_same = jnp.where(same_chunk & before, _dot_nt(q_mid[:, ks], k_mid[:, ks]), 0.0)
        blocks = []
        for c in range(n_chunks):
            a_c = a_same[rows_of(c)]
            if c in q_off:
                a_c = a_c + jnp.where(off_mask[c], _dot_nt(q_off[c][:, ks], k_off[c][:, ks]), 0.0)
            blocks.append(a_c)
        a = jnp.concatenate(blocks, axis=0).astype(BF16)
        st = st_ref[h]
        out_ref[0, :, vs] = _dot_nt(q_state[:, ks], st.astype(BF16)) + _dot(a, v_h)
        st_ref[h] = st * decay[:, ks] + _dot_tn(v_h, k_state[:, ks])


def _gla_scan(q, k, v, la, lay, reverse):
    b, n = lay.batch, lay.n
    n_tiles, ctx_tiles = n // GLA_TILE, lay.n_ctx // GLA_TILE

    def tidx(j):
        return _scan_chunk_index(reverse, j, n_tiles, ctx_tiles)

    kern = functools.partial(_gla_scan_kernel, reverse=reverse)
    return pl.pallas_call(
        kern,
        grid=(b, n_tiles),
        in_specs=[
            pl.BlockSpec((1, GLA_TILE, GLA_QK), lambda bb, j: (bb, tidx(j), 0)),
            pl.BlockSpec((1, GLA_TILE, GLA_QK), lambda bb, j: (bb, tidx(j), 0)),
            pl.BlockSpec((1, GLA_TILE, D_MODEL), lambda bb, j: (bb, tidx(j), 0)),
            pl.BlockSpec((1, GLA_TILE, GLA_QK), lambda bb, j: (bb, tidx(j), 0)),
        ],
        out_specs=pl.BlockSpec((1, GLA_TILE, D_MODEL), lambda bb, j: (bb, tidx(j), 0)),
        out_shape=jax.ShapeDtypeStruct((b, n, D_MODEL), F32),
        scratch_shapes=[pltpu.VMEM((GLA_HEADS, GLA_DV, GLA_DK), F32)],
        compiler_params=_cparams(("parallel", "arbitrary")),
        name="gla_scan_bwd" if reverse else "gla_scan_fwd",
    )(q, k, v, la)


def _readout_kernel(*refs, mode):
    if mode == "attn":
        a_ref, wout_ref = refs[:2]
        rest = refs[2:]
        y_in = a_ref[0]
    else:
        a_ref, b_ref, gsrc_ref, ng_ref, wout_ref = refs[:5]
        rest = refs[5:]
        hs = a_ref[0] + b_ref[0]
        if mode == "mlstm":
            normed = []
            for i in range(MLSTM_HEADS):
                xs = hs[i * MLSTM_DV:(i + 1) * MLSTM_DV, :]
                ms = jnp.mean(xs * xs, axis=0, keepdims=True)
                normed.append(xs * lax.rsqrt(ms + NORM_EPS))
            y_in = (jnp.concatenate(normed, axis=0) * ng_ref[...] * _sigmoid(gsrc_ref[0])).T
        else:
            gsrc = gsrc_ref[0]
            y_in = _head_rms(hs, GLA_DV) * ng_ref[...] * (gsrc * _sigmoid(gsrc))
        y_in = y_in.astype(BF16)
    (h_ref, mod_ref, n2g_ref, wrh_ref, wrl_ref, br_ref,
     hnew_ref, v_ref, e_ref, gate_ref, rank_ref, cnt_ref, carry_ref) = rest

    first = (pl.program_id(0) == 0) & (pl.program_id(1) == 0)

    @pl.when(first)
    def _():
        carry_ref[...] = jnp.zeros_like(carry_ref)

    mod = mod_ref[0]
    hn = h_ref[0] + mod[2:3] * _dot(y_in, wout_ref[...])
    hnew_ref[0] = hn
    v = _norm_mod(hn, n2g_ref[...], mod[3:4], mod[4:5])
    for c in range(ROW_TILE):
        v_ref[0, pl.ds(c, v.shape[0], stride=ROW_TILE), :] = v[:, c * LANES:(c + 1) * LANES]

    v_hi = v.astype(BF16)
    v_lo = (v - v_hi.astype(F32)).astype(BF16)
    w_hi = wrh_ref[...]
    logits = _dot(v_hi, w_hi) + _dot(v_lo, w_hi) + _dot(v_hi, wrl_ref[...]) + br_ref[...]

    tm = logits.shape[0]
    lane = lax.broadcasted_iota(I32, logits.shape, 1)
    lane_f = lane.astype(F32)
    work = logits
    top_e = jnp.zeros(logits.shape, F32)
    top_p = jnp.zeros(logits.shape, F32)
    onehot = jnp.zeros(logits.shape, F32)
    m0 = None
    hits = []
    for jx in range(TOP_K):
        mx = jnp.max(work, axis=-1, keepdims=True)
        idx = jnp.min(jnp.where(work == mx, lane_f, float(LANES)), axis=-1, keepdims=True)
        if jx == 0:
            m0 = mx
        hit = lane_f == idx
        hits.append(hit)
        top_e = jnp.where(lane == jx, idx, top_e)
        top_p = jnp.where(lane == jx, jnp.exp(mx - m0), top_p)
        onehot = jnp.where(hit, 1.0, onehot)
        work = jnp.where(hit, -jnp.inf, work)
    e_ref[0] = top_e.astype(I32)
    gate_ref[0] = top_p / jnp.sum(top_p, axis=-1, keepdims=True)

    r = lax.broadcasted_iota(I32, (tm, tm), 0)
    c = lax.broadcasted_iota(I32, (tm, tm), 1)
    tril = (c <= r).astype(BF16)
    cum = _dot(tril, onehot.astype(BF16))
    carry = carry_ref[0:1, :]
    rank_all = carry + cum - onehot
    rank_sel = jnp.zeros(logits.shape, F32)
    for jx in range(TOP_K):
        picked = jnp.sum(jnp.where(hits[jx], rank_all, 0.0), axis=-1, keepdims=True)
        rank_sel = jnp.where(lane == jx, picked, rank_sel)
    rank_ref[0] = rank_sel.astype(I32)
    total = carry + cum[tm - 1:tm, :]
    carry_ref[...] = jnp.broadcast_to(total, carry_ref.shape)
    cnt_ref[...] = jnp.broadcast_to(total, cnt_ref.shape)


def _readout(mode, mixer_outs, w_out, h, mod_l, norm2_g, router_w, router_b, lay):
    d = D_MODEL
    b, n = lay.batch, lay.n
    wr = jnp.zeros((d, LANES), F32).at[:, :N_EXPERTS].set(router_w)
    wr_hi = wr.astype(BF16)
    wr_lo = (wr - wr_hi.astype(F32)).astype(BF16)
    br = jnp.full((1, LANES), NEG_BIG, F32).at[0, :N_EXPERTS].set(router_b)
    if mode == "attn":
        (attn_o,) = mixer_outs
        head_in = [attn_o]
        head_specs = [_tok_spec(d, lay)]
    elif mode == "mlstm":
        h_dir, o_gate, norm_g = mixer_outs
        head_in = [h_dir, h_dir, o_gate, norm_g.reshape(d, 1)]
        head_specs = [pl.BlockSpec((None, 1, d, TOKEN_TILE), lambda bb, t: (0, bb, 0, t)),
                      pl.BlockSpec((None, 1, d, TOKEN_TILE), lambda bb, t: (1, bb, 0, t)),
                      _feat_spec(d), _full_spec((d, 1))]
    else:
        o_f, o_b, g_gate, norm_g = mixer_outs
        head_in = [o_f, o_b, g_gate, norm_g.reshape(1, d)]
        head_specs = [_tok_spec(d, lay), _tok_spec(d, lay), _tok_spec(d, lay), _full_spec((1, d))]
    kern = functools.partial(_readout_kernel, mode=mode)
    return pl.pallas_call(
        kern,
        grid=(b, lay.tiles),
        in_specs=head_specs + [
            _full_spec((d, d)), _tok_spec(d, lay), _mod_spec(lay), _full_spec((1, d)),
            _full_spec((d, LANES)), _full_spec((d, LANES)), _full_spec((1, LANES)),
        ],
        out_specs=[_tok_spec(d, lay), pl.BlockSpec((1, TOKEN_TILE * ROW_TILE, LANES), lambda bb, t: (bb, t, 0)),
                   _tok_spec(LANES, lay), _tok_spec(LANES, lay),
                   _tok_spec(LANES, lay), _full_spec((SUBLANES, LANES))],
        out_shape=[
            jax.ShapeDtypeStruct((b, n, d), F32),
            jax.ShapeDtypeStruct((b, n * ROW_TILE, LANES), F32),
            jax.ShapeDtypeStruct((b, n, LANES), I32),
            jax.ShapeDtypeStruct((b, n, LANES), F32),
            jax.ShapeDtypeStruct((b, n, LANES), I32),
            jax.ShapeDtypeStruct((SUBLANES, LANES), F32),
        ],
        scratch_shapes=[pltpu.VMEM((SUBLANES, LANES), F32)],
        compiler_params=_cparams(("arbitrary", "arbitrary")),
        name="readout_" + mode,
    )(*head_in, w_out.astype(BF16), h, mod_l, norm2_g.reshape(1, d), wr_hi, wr_lo, br)


def _slot(ps_ref, e_ref, r_ref, i):
    return ps_ref[e_ref[0, 0, i]] + r_ref[0, 0, i]


def _row_copies_wait(src_block, dst_rows, sem):
    for _ in range(TOP_K):
        pltpu.make_async_copy(src_block, dst_rows, sem).wait()


def _zero_fill(zs_ref, nu_ref, xb_ref, zbuf_ref, sem):
    zbuf_ref[...] = jnp.zeros_like(zbuf_ref)
    blk_rows = zbuf_ref.shape[0]
    n_blocks = xb_ref.shape[0] // blk_rows
    for e in range(N_EXPERTS):
        start = pl.multiple_of(zs_ref[e] * ROW_TILE, ROW_TILE)
        pltpu.make_async_copy(zbuf_ref, xb_ref.at[pl.ds(start, blk_rows), :], sem).start()
    for e in range(N_EXPERTS):
        pltpu.make_async_copy(zbuf_ref, xb_ref.at[pl.ds(0, blk_rows), :], sem).wait()
    for i in range(n_blocks - N_EXPERTS, n_blocks):

        @pl.when(i >= nu_ref[0])
        def _():
            cp = pltpu.make_async_copy(zbuf_ref, xb_ref.at[pl.ds(i * blk_rows, blk_rows), :], sem)
            cp.start()
            cp.wait()


def _dispatch_kernel(e_ref, r_ref, ps_ref, zs_ref, nu_ref, v_ref, xb_ref, zbuf_ref, sem):
    rows = v_ref.shape[1] // ROW_TILE

    @pl.when((pl.program_id(0) == 0) & (pl.program_id(1) == 0))
    def _():
        _zero_fill(zs_ref, nu_ref, xb_ref, zbuf_ref, sem)

    def body(r, carry):
        src = v_ref.at[0, pl.ds(pl.multiple_of(r * ROW_TILE, ROW_TILE), ROW_TILE), :]
        for jx in range(TOP_K):
            slot = _slot(ps_ref, e_ref, r_ref, r * TOP_K + jx)
            dst = xb_ref.at[pl.ds(pl.multiple_of(slot * ROW_TILE, ROW_TILE), ROW_TILE), :]
            pltpu.make_async_copy(src, dst, sem).start(priority=jx % DMA_THREADS)
        return carry

    lax.fori_loop(0, rows, body, 0, unroll=DMA_UNROLL)
    _row_copies_wait(v_ref.at[0], xb_ref.at[pl.ds(0, rows * ROW_TILE), :], sem)


def _assign_spec(lay):
    tiles = lay.tiles
    return pl.BlockSpec((1, 1, TOKEN_TILE * TOP_K), lambda bb, t: (bb * tiles + t, 0, 0), memory_space=pltpu.SMEM)


def _dispatch(e_c, r_c, pad_starts, zero_start, n_used, v_rows, cap, lay):
    return pl.pallas_call(
        _dispatch_kernel,
        grid=(lay.batch, lay.tiles),
        in_specs=[
            _assign_spec(lay), _assign_spec(lay),
            pl.BlockSpec(memory_space=pltpu.SMEM),
            pl.BlockSpec(memory_space=pltpu.SMEM),
            pl.BlockSpec(memory_space=pltpu.SMEM),
            pl.BlockSpec((1, TOKEN_TILE * ROW_TILE, LANES), lambda bb, t: (bb, t, 0)),
        ],
        out_specs=pl.BlockSpec(memory_space=pl.ANY),
        out_shape=jax.ShapeDtypeStruct((cap * ROW_TILE, LANES), F32),
        scratch_shapes=[pltpu.VMEM((MOE_BLOCK * ROW_TILE, LANES), F32), pltpu.SemaphoreType.DMA(())],
        compiler_params=_cparams(("arbitrary", "arbitrary")),
        name="moe_dispatch",
    )(e_c, r_c, pad_starts, zero_start, n_used, v_rows)


def _expert_kernel(be_ref, nu_ref, x_ref, wgu_ref, bgu_ref, wd_ref, bd_ref, y_ref, wgu_bf_ref, wd_bf_ref):
    i = pl.program_id(0)
    e = be_ref[i]
    used = i < nu_ref[0]
    new_expert = (i == 0) | (e != be_ref[jnp.maximum(i - 1, 0)])

    @pl.when(new_expert)
    def _():
        wgu_bf_ref[...] = wgu_ref[0, 0].astype(BF16)
        wd_bf_ref[...] = wd_ref[0, 0].astype(BF16)

    @pl.when(jnp.logical_not(used))
    def _():
        y_ref[...] = jnp.zeros_like(y_ref)

    @pl.when(used)
    def _():
        x = jnp.concatenate([x_ref[pl.ds(c, MOE_BLOCK, stride=ROW_TILE), :] for c in range(ROW_TILE)], axis=1)
        gu = _dot(x.astype(BF16), wgu_bf_ref[...]) + bgu_ref[0, 0]
        g = jnp.minimum(gu[:, :D_EXPERT], SWIGLU_LIMIT)
        lin = jnp.clip(gu[:, D_EXPERT:], -SWIGLU_LIMIT, SWIGLU_LIMIT)
        hdn = g * _sigmoid(SWIGLU_ALPHA * g) * (lin + 1.0)
        y = _dot(hdn.astype(BF16), wd_bf_ref[...]) + bd_ref[0, 0]
        for c in range(ROW_TILE):
            y_ref[pl.ds(c, MOE_BLOCK, stride=ROW_TILE), :] = y[:, c * LANES:(c + 1) * LANES]


def _experts(block_e, n_used, xb, w_gu, b_gu, w_d, b_d, layer):
    d, de = D_MODEL, D_EXPERT
    depth = w_gu.shape[0]
    blk_rows = MOE_BLOCK * ROW_TILE
    n_blocks = xb.shape[0] // blk_rows

    def row_map(i, be, nu):
        return (jnp.minimum(i, nu[0] - 1), 0)

    def exp_map(i, be, nu):
        return (layer, be[i], 0, 0)

    grid_spec = pltpu.PrefetchScalarGridSpec(
        num_scalar_prefetch=2,
        grid=(n_blocks,),
        in_specs=[
            pl.BlockSpec((blk_rows, LANES), row_map),
            pl.BlockSpec((1, 1, d, 2 * de), exp_map),
            pl.BlockSpec((1, 1, 1, 2 * de), exp_map),
            pl.BlockSpec((1, 1, de, d), exp_map),
            pl.BlockSpec((1, 1, 1, d), exp_map),
        ],
        out_specs=pl.BlockSpec((blk_rows, LANES), lambda i, be, nu: (i, 0)),
        scratch_shapes=[pltpu.VMEM((d, 2 * de), BF16), pltpu.VMEM((de, d), BF16)],
    )
    return pl.pallas_call(
        _expert_kernel,
        grid_spec=grid_spec,
        out_shape=jax.ShapeDtypeStruct(xb.shape, F32),
        compiler_params=_cparams(("arbitrary",), vmem=EXPERT_VMEM_LIMIT),
        name="moe_experts",
    )(block_e, n_used, xb, w_gu, b_gu.reshape(depth, N_EXPERTS, 1, 2 * de), w_d,
      b_d.reshape(depth, N_EXPERTS, 1, d))


def _combine_kernel(e_ref, r_ref, en_ref, rn_ref, gate_ref, ps_ref, yb_ref, h_ref, mod_ref, out_ref,
                    ybuf_ref, acc_ref, sems, *, tiles_per_sample, n_tiles):
    rows = h_ref.shape[1]
    tile = pl.program_id(0) * tiles_per_sample + pl.program_id(1)
    buf = tile % 2

    def issue_rows(e_src, r_src, into):
        def issue(r, carry):
            base = pl.multiple_of(r * ROW_TILE, ROW_TILE)
            for jx in range(TOP_K):
                slot = _slot(ps_ref, e_src, r_src, r * TOP_K + jx)
                src = yb_ref.at[pl.ds(pl.multiple_of(slot * ROW_TILE, ROW_TILE), ROW_TILE), :]
                pltpu.make_async_copy(src, ybuf_ref.at[into, jx, pl.ds(base, ROW_TILE), :], sems.at[into]).start(
                    priority=jx % DMA_THREADS)
            return carry

        lax.fori_loop(0, rows, issue, 0, unroll=DMA_UNROLL)

    @pl.when(tile == 0)
    def _():
        issue_rows(e_ref, r_ref, 0)

    @pl.when(tile + 1 < n_tiles)
    def _():
        issue_rows(en_ref, rn_ref, 1 - buf)

    _row_copies_wait(yb_ref.at[pl.ds(0, rows * ROW_TILE), :], ybuf_ref.at[buf, 0], sems.at[buf])

    def accumulate(r, carry):
        base = pl.multiple_of(r * ROW_TILE, ROW_TILE)
        a = gate_ref[0, 0, r * TOP_K] * ybuf_ref[buf, 0, pl.ds(base, ROW_TILE), :]
        for jx in range(1, TOP_K):
            a = a + gate_ref[0, 0, r * TOP_K + jx] * ybuf_ref[buf, jx, pl.ds(base, ROW_TILE), :]
        acc_ref[pl.ds(base, ROW_TILE), :] = a
        return carry

    lax.fori_loop(0, rows, accumulate, 0, unroll=DMA_UNROLL)
    for c in range(ROW_TILE):
        cs = slice(c * LANES, (c + 1) * LANES)
        out_ref[0, :, cs] = h_ref[0, :, cs] + mod_ref[0, 5:6, cs] * acc_ref[pl.ds(c, rows, stride=ROW_TILE), :]


def _combine(e_c, r_c, gate_c, pad_starts, yb, h, mod_l, lay):
    d = D_MODEL
    tiles = lay.tiles
    n_tiles = lay.batch * tiles
    next_spec = pl.BlockSpec((1, 1, TOKEN_TILE * TOP_K),
                             lambda bb, t: (jnp.minimum(bb * tiles + t + 1, n_tiles - 1), 0, 0),
                             memory_space=pltpu.SMEM)
    return pl.pallas_call(
        functools.partial(_combine_kernel, tiles_per_sample=tiles, n_tiles=n_tiles),
        grid=(lay.batch, tiles),
        in_specs=[_assign_spec(lay), _assign_spec(lay), next_spec, next_spec, _assign_spec(lay),
                  pl.BlockSpec(memory_space=pltpu.SMEM),
                  pl.BlockSpec(memory_space=pl.ANY), _tok_spec(d, lay), _mod_spec(lay)],
        out_specs=_tok_spec(d, lay),
        out_shape=jax.ShapeDtypeStruct((lay.batch, lay.n, d), F32),
        scratch_shapes=[pltpu.VMEM((2, TOP_K, TOKEN_TILE * ROW_TILE, LANES), F32),
                        pltpu.VMEM((TOKEN_TILE * ROW_TILE, LANES), F32), pltpu.SemaphoreType.DMA((2,))],
        compiler_params=_cparams(("arbitrary", "arbitrary")),
        name="moe_combine",
    )(e_c, r_c, e_c, r_c, gate_c, pad_starts, yb, h, mod_l)


def _moe(h, v_rows, top_e, gate, rank, counts, mod_l, w_gu, b_gu, w_d, b_d, layer, lay):
    n_tokens = lay.batch * lay.n
    n_assign = n_tokens * TOP_K
    cap = (-(-n_assign // MOE_BLOCK) + N_EXPERTS) * MOE_BLOCK
    n_blocks = cap // MOE_BLOCK
    cnt = counts[0, :N_EXPERTS].astype(I32)
    padded = (cnt + MOE_BLOCK - 1) // MOE_BLOCK * MOE_BLOCK
    pad_ends = jnp.cumsum(padded)
    pad_starts = pad_ends - padded
    n_used = (pad_ends[-1] // MOE_BLOCK).astype(I32).reshape(1)
    blk = jnp.minimum(jnp.arange(n_blocks, dtype=I32), n_used[0] - 1) * MOE_BLOCK
    block_e = jnp.minimum(jnp.sum((pad_ends[None, :] <= blk[:, None]).astype(I32), axis=1), N_EXPERTS - 1)
    pad_starts = pad_starts.astype(I32)
    zero_start = jnp.minimum(pad_starts + cnt, cap - MOE_BLOCK).astype(I32)

    def per_assignment(a):
        return a[:, :, :TOP_K].reshape(lay.batch * lay.tiles, 1, TOKEN_TILE * TOP_K)

    e_c, r_c, gate_c = per_assignment(top_e), per_assignment(rank), per_assignment(gate)
    xb = _dispatch(e_c, r_c, pad_starts, zero_start, n_used, v_rows, cap, lay)
    yb = _experts(block_e, n_used, xb, w_gu, b_gu, w_d, b_d, layer)
    return _combine(e_c, r_c, gate_c, pad_starts, yb, h, mod_l, lay)


def kernel(x, c, ctx, c_ctx, norm1_g, norm2_g, mod_w, mod_b,
           mlstm_w_in, mlstm_conv_w, mlstm_gate_b, mlstm_out_norm_g, mlstm_w_out,
           attn_w_in, attn_q_norm_g, attn_k_norm_g, attn_w_out,
           gla_w_in, gla_alpha_w2, gla_alpha_b, gla_out_norm_g, gla_w_out,
           router_w, router_b, moe_w_gu, moe_b_gu, moe_w_down, moe_b_down):
    batch, n_lat, d = x.shape
    n_ctx = ctx.shape[1]
    depth = norm1_g.shape[0]
    assert d == D_MODEL
    lay = _Layout(batch, n_ctx, n_lat)

    cond_rows = -(-(batch + 1) // SUBLANES) * SUBLANES
    cond = jnp.zeros((cond_rows, d), F32).at[:batch].set(c).at[batch].set(c_ctx)
    mod_all = _modulation(cond, mod_w, mod_b)

    h = jnp.concatenate([ctx, x], axis=1)
    for layer in range(depth):
        kind, j = layer % 3, layer // 3
        mod_l = mod_all[layer]
        if kind == 0:
            qk_pre, v_t, o_gate, gcol, grow = _mlstm_in(h, norm1_g[layer], mod_l, mlstm_w_in[j],
                                                        mlstm_gate_b[j], lay)
            q_t, k_m = _mlstm_conv(qk_pre, mlstm_conv_w[j], lay)
            h_dir = _mlstm_scan(q_t, k_m, v_t, gcol, grow, lay)
            mixer = ("mlstm", (h_dir, o_gate, mlstm_out_norm_g[j]), mlstm_w_out[j])
        elif kind == 1:
            q_a, k_a, v_a = _gqa_in(h, norm1_g[layer], mod_l, attn_w_in[j], attn_q_norm_g[j],
                                    attn_k_norm_g[j], lay)
            mixer = ("attn", (_attention(q_a, k_a, v_a, lay),), attn_w_out[j])
        else:
            q_g, k_g, v_g, g_gate, la_f, la_b = _gla_in(h, norm1_g[layer], mod_l, gla_w_in[j],
                                                        gla_alpha_w2[j], gla_alpha_b[j], lay)
            o_f = _gla_scan(q_g, k_g, v_g, la_f, lay, reverse=False)
            o_b = _gla_scan(q_g, k_g, v_g, la_b, lay, reverse=True)
            mixer = ("gla", (o_f, o_b, g_gate, gla_out_norm_g[j]), gla_w_out[j])
        mode, mixer_outs, w_out = mixer
        h, v_moe, top_e, gate, rank, counts = _readout(mode, mixer_outs, w_out, h, mod_l, norm2_g[layer],
                                                       router_w[layer], router_b[layer], lay)
        h = _moe(h, v_moe, top_e, gate, rank, counts, mod_l,
                 moe_w_gu, moe_b_gu, moe_w_down, moe_b_down, layer, lay)
    return h[:, n_ctx:]
```

```python
import functools

import jax
import jax.numpy as jnp
from jax import lax
from jax.experimental import pallas as pl
from jax.experimental.pallas import tpu as pltpu

F32 = jnp.float32
BF16 = jnp.bfloat16
I32 = jnp.int32
HIGHEST = lax.Precision.HIGHEST

D_MODEL = 1024
GRID_W = 64
NORM_EPS = 1e-6

MLSTM_HEADS = 8
MLSTM_DQK = 64
MLSTM_DV = 128
MLSTM_QK = MLSTM_HEADS * MLSTM_DQK

ATTN_HEADS = 8
ATTN_KV_HEADS = 4
ATTN_GROUP = ATTN_HEADS // ATTN_KV_HEADS
ATTN_HEAD_DIM = 128
ROPE_THETA = 10000.0

GLA_HEADS = 4
GLA_DK = 128
GLA_DV = 256
GLA_GATE_RANK = 16
GLA_TAU = 16.0
GLA_QK = GLA_HEADS * GLA_DK

N_EXPERTS = 32
TOP_K = 4
D_EXPERT = D_MODEL
SWIGLU_LIMIT = 7.0
SWIGLU_ALPHA = 1.702

LANES = 128
SUBLANES = 8
TOKEN_TILE = 256
MLSTM_CHUNK = 256
GLA_TILE = 128
GLA_CHUNK = 32
ATTN_KV_CHUNK = 256
ATTN_Q_SCALE = ATTN_HEAD_DIM ** -0.5 * 1.4426950408889634
MOE_BLOCK = 512
ROW_TILE = D_MODEL // LANES
assert ROW_TILE == SUBLANES
DMA_UNROLL = 8
DMA_THREADS = 2
VMEM_LIMIT = 48 * 1024 * 1024
EXPERT_VMEM_LIMIT = 56 * 1024 * 1024
NEG_BIG = -1e30


def _cparams(semantics, vmem=VMEM_LIMIT):
    return pltpu.CompilerParams(dimension_semantics=semantics, vmem_limit_bytes=vmem)


def _log_sigmoid(x):
    return jnp.minimum(x, 0.0) - jnp.log(1.0 + jnp.exp(-jnp.abs(x)))


def _sigmoid(x):
    return 1.0 / (1.0 + jnp.exp(-x))


def _norm_mod(x, g, shift, scale):
    ms = jnp.mean(x * x, axis=-1, keepdims=True)
    y = x * lax.rsqrt(ms + NORM_EPS) * g
    return y * (1.0 + scale) + shift


def _head_rms(x, head_dim):
    outs = []
    for i in range(x.shape[1] // head_dim):
        xs = x[:, i * head_dim:(i + 1) * head_dim]
        ms = jnp.mean(xs * xs, axis=-1, keepdims=True)
        outs.append(xs * lax.rsqrt(ms + NORM_EPS))
    return jnp.concatenate(outs, axis=1)


def _dot(a, b):
    return jnp.dot(a, b, preferred_element_type=F32)


def _dot_nt(a, b):
    return lax.dot_general(a, b, (((1,), (1,)), ((), ())), preferred_element_type=F32)


def _dot_tn(a, b):
    return lax.dot_general(a, b, (((0,), (0,)), ((), ())), preferred_element_type=F32)


def _mod_kernel(c_ref, w_ref, b_ref, o_ref):
    x = c_ref[...]
    x = x * _sigmoid(x)
    o_ref[0, 0] = _dot(x.astype(BF16), w_ref[0].astype(BF16)) + b_ref[0, 0]


def _modulation(cond_rows, mod_w, mod_b):
    depth, d, _ = mod_w.shape
    rows = cond_rows.shape[0]
    out = pl.pallas_call(
        _mod_kernel,
        grid=(depth, 6),
        in_specs=[
            pl.BlockSpec((rows, d), lambda l, j: (0, 0)),
            pl.BlockSpec((1, d, d), lambda l, j: (l, 0, j)),
            pl.BlockSpec((1, 1, 1, d), lambda l, j: (l, j, 0, 0)),
        ],
        out_specs=pl.BlockSpec((1, 1, rows, d), lambda l, j: (l, j, 0, 0)),
        out_shape=jax.ShapeDtypeStruct((depth, 6, rows, d), F32),
        compiler_params=_cparams(("parallel", "parallel")),
        name="modulation",
    )(cond_rows, mod_w, mod_b.reshape(depth, 6, 1, d))
    return out.transpose(0, 2, 1, 3)


class _Layout:
    def __init__(self, batch, n_ctx, n_lat):
        self.batch = batch
        self.n_ctx = n_ctx
        self.n_lat = n_lat
        self.n = n_ctx + n_lat
        assert n_ctx % TOKEN_TILE == 0 and n_lat % TOKEN_TILE == 0
        assert n_ctx % MLSTM_CHUNK == 0 and n_lat % MLSTM_CHUNK == 0
        assert n_ctx % GLA_TILE == 0 and n_lat % GLA_TILE == 0
        self.tiles = self.n // TOKEN_TILE
        self.ctx_tiles = n_ctx // TOKEN_TILE
        self.ctx_row = batch

    def mod_index(self, b, t):
        return jnp.where(t < self.ctx_tiles, self.ctx_row, b)


def _tok_spec(width, lay):
    return pl.BlockSpec((1, TOKEN_TILE, width), lambda b, t: (b, t, 0))


def _feat_spec(features):
    return pl.BlockSpec((1, features, TOKEN_TILE), lambda b, t: (b, 0, t))


def _full_spec(shape):
    nd = len(shape)
    return pl.BlockSpec(shape, lambda b, t: (0,) * nd)


def _mod_spec(lay):
    return pl.BlockSpec((1, 6, D_MODEL), lambda b, t: (lay.mod_index(b, t), 0, 0))


def _mlstm_in_kernel(h_ref, g_ref, mod_ref, w_ref, wvot_ref, wg_ref, wgt_ref, gb_ref, gbt_ref,
                     qk_ref, vt_ref, ot_ref, gc_ref, gr_ref):
    mod = mod_ref[0]
    u = _norm_mod(h_ref[0], g_ref[...], mod[0:1], mod[1:2]).astype(BF16)
    qk_ref[0] = _dot(u, w_ref[...])
    vo_t = _dot_nt(wvot_ref[...], u)
    vt_ref[0] = vo_t[:D_MODEL].astype(BF16)
    ot_ref[0] = vo_t[D_MODEL:]
    gc = _dot(u, wg_ref[...]) + gb_ref[...]
    lane = lax.broadcasted_iota(I32, gc.shape, 1)
    gc_ref[0] = jnp.where(((lane >> 3) & 1) == 1, _log_sigmoid(gc), gc)
    gr = _dot_nt(wgt_ref[...], u) + gbt_ref[...]
    sub = lax.broadcasted_iota(I32, gr.shape, 0)
    gr_ref[0] = jnp.where(((sub >> 3) & 1) == 1, _log_sigmoid(gr), gr)


def _mlstm_in(h, norm_g, mod_l, w_in, gate_b, lay):
    d = D_MODEL
    n_qk = 2 * MLSTM_QK
    n_main = n_qk + 2 * d
    n_gate = 4 * MLSTM_HEADS
    w_qk = w_in[:, :n_qk].astype(BF16)
    w_vo_t = w_in[:, n_qk:n_main].T.astype(BF16)
    w_gate = w_in[:, n_main:]
    wg = jnp.zeros((d, LANES), F32).at[:, :n_gate].set(w_gate).astype(BF16)
    wgt = w_gate.T.astype(BF16)
    gb = jnp.zeros((1, LANES), F32).at[0, :n_gate].set(gate_b.reshape(-1))
    gbt = gate_b.reshape(n_gate, 1)
    b, n = lay.batch, lay.n
    return pl.pallas_call(
        _mlstm_in_kernel,
        grid=(b, lay.tiles),
        in_specs=[
            _tok_spec(d, lay), _full_spec((1, d)), _mod_spec(lay),
            _full_spec((d, n_qk)), _full_spec((2 * d, d)), _full_spec((d, LANES)), _full_spec((n_gate, d)),
            _full_spec((1, LANES)), _full_spec((n_gate, 1)),
        ],
        out_specs=[
            _tok_spec(n_qk, lay), _feat_spec(d), _feat_spec(d), _tok_spec(LANES, lay), _feat_spec(n_gate),
        ],
        out_shape=[
            jax.ShapeDtypeStruct((b, n, n_qk), F32),
            jax.ShapeDtypeStruct((b, d, n), BF16),
            jax.ShapeDtypeStruct((b, d, n), F32),
            jax.ShapeDtypeStruct((b, n, LANES), F32),
            jax.ShapeDtypeStruct((b, n_gate, n), F32),
        ],
        compiler_params=_cparams(("parallel", "parallel")),
        name="mlstm_in",
    )(h, norm_g.reshape(1, d), mod_l, w_qk, w_vo_t, wg, wgt, gb, gbt)


def _mlstm_conv_kernel(x_ref, xp_ref, xn_ref, cw_ref, qt_ref, k_ref, *, tiles, ctx_tiles):
    t = pl.program_id(1)
    x = x_ref[0]
    rows = x.shape[0]
    seg_first = (t == 0) | (t == ctx_tiles)
    seg_last = (t == ctx_tiles - 1) | (t == tiles - 1)
    prev_row = jnp.where(seg_first, 0.0, xp_ref[0, SUBLANES - 1:SUBLANES, :])
    next_row = jnp.where(seg_last, 0.0, xn_ref[0, 0:1, :])
    rid = lax.broadcasted_iota(I32, x.shape, 0)
    x_m1 = jnp.where(rid == 0, prev_row, pltpu.roll(x, 1, 0))
    x_p1 = jnp.where(rid == rows - 1, next_row, pltpu.roll(x, rows - 1, 0))
    cw = cw_ref[...]
    y = cw[0:1] * x_m1 + cw[1:2] * x + cw[2:3] * x_p1
    qk = y * _sigmoid(y)
    qt_ref[0] = (qk[:, :MLSTM_QK] * MLSTM_DQK ** -0.5).T.astype(BF16)
    k_ref[0] = qk[:, MLSTM_QK:].astype(BF16)


def _mlstm_conv(qk_pre, conv_w, lay):
    b, n = lay.batch, lay.n
    n_qk = 2 * MLSTM_QK
    halo = TOKEN_TILE // SUBLANES
    n_halo = n // SUBLANES
    kern = functools.partial(_mlstm_conv_kernel, tiles=lay.tiles, ctx_tiles=lay.ctx_tiles)
    return pl.pallas_call(
        kern,
        grid=(b, lay.tiles),
        in_specs=[
            _tok_spec(n_qk, lay),
            pl.BlockSpec((1, SUBLANES, n_qk), lambda bb, t: (bb, jnp.maximum(t * halo - 1, 0), 0)),
            pl.BlockSpec((1, SUBLANES, n_qk), lambda bb, t: (bb, jnp.minimum((t + 1) * halo, n_halo - 1), 0)),
            _full_spec((3, n_qk)),
        ],
        out_specs=[_feat_spec(MLSTM_QK), _tok_spec(MLSTM_QK, lay)],
        out_shape=[
            jax.ShapeDtypeStruct((b, MLSTM_QK, n), BF16),
            jax.ShapeDtypeStruct((b, n, MLSTM_QK), BF16),
        ],
        compiler_params=_cparams(("parallel", "parallel")),
        name="mlstm_conv",
    )(qk_pre, qk_pre, qk_pre, conv_w)


def _scan_chunk_index(reverse, j, n_chunks, ctx_chunks):
    back = jnp.where(j < ctx_chunks, ctx_chunks - 1 - j, n_chunks - 1 - (j - ctx_chunks))
    return jnp.where(reverse, back, j)


def _mlstm_scan_kernel(qt_ref, k_ref, vt_ref, gc_ref, gr_ref, out_ref, st_ref, m_ref):
    L = MLSTM_CHUNK
    dk, dv, nh = MLSTM_DQK, MLSTM_DV, MLSTM_HEADS
    d = pl.program_id(1)
    j = pl.program_id(2)

    @pl.when(j == 0)
    def _():
        st_ref[...] = jnp.zeros_like(st_ref)
        m_ref[...] = jnp.zeros_like(m_ref)

    q_t = qt_ref[0]
    k = k_ref[0]
    v_t = vt_ref[0]

    gcol = gc_ref[0]
    gcol = jnp.where(d == 0, gcol, pltpu.roll(gcol, LANES - 2 * nh, 1))
    grow = gr_ref[0, pl.ds(pl.multiple_of(d * 2 * nh, 2 * nh), 2 * nh), :]

    row = lax.broadcasted_iota(I32, (L, L), 0)
    col = lax.broadcasted_iota(I32, (L, L), 1)
    sgn = 1 - 2 * d
    mask_t = sgn * (col - row) >= 0
    cum_rows = jnp.dot(grow, mask_t.astype(F32), precision=HIGHEST, preferred_element_type=F32)
    cum_cols = jnp.dot((sgn * (row - col) >= 0).astype(F32), gcol, precision=HIGHEST,
                       preferred_element_type=F32)
    b_minus_i = cum_cols - pltpu.roll(gcol, nh, 1)
    b_tot = jnp.sum(grow[nh:], axis=-1, keepdims=True)

    ones_row = (lax.broadcasted_iota(I32, (LANES, L), 0) == 0).astype(BF16)
    for h in range(nh):
        b_t = cum_rows[nh + h:nh + h + 1, :]
        li = grow[h:h + 1, :]
        m_prev = m_ref[h, 0:1, 0:1]
        dmat = jnp.where(mask_t, b_t - b_minus_i[:, nh + h:nh + h + 1], -jnp.inf)
        inter = b_t + m_prev
        m_t = jnp.maximum(inter, jnp.max(dmat, axis=0, keepdims=True))
        w = jnp.exp(dmat - m_t)
        a_inter = jnp.exp(inter - m_t)
        qt_h = q_t[h * dk:(h + 1) * dk, :]
        k_h = k[:, h * dk:(h + 1) * dk]
        s = (_dot(k_h, qt_h) * w).astype(BF16)
        vt_aug = jnp.concatenate([v_t[h * dv:(h + 1) * dv, :], ones_row], axis=0)
        st = st_ref[h]
        r = a_inter * _dot(st.astype(BF16), qt_h) + _dot(vt_aug, s)
        den = r[dv:dv + 1, :]
        out_ref[0, 0, h * dv:(h + 1) * dv, :] = r[:dv] / jnp.maximum(jnp.abs(den), jnp.exp(-m_t))
        b_last = b_tot[h:h + 1, :]
        g = b_last - b_t + li
        m_new = jnp.maximum(b_last + m_prev, jnp.max(g, axis=-1, keepdims=True))
        decay = jnp.exp(b_last + m_prev - m_new)
        wg = jnp.exp(g - m_new)
        st_ref[h] = decay * st + _dot((vt_aug.astype(F32) * wg).astype(BF16), k_h)
        m_ref[h] = jnp.broadcast_to(m_new, (SUBLANES, LANES))


def _mlstm_scan(q_t, k, v_t, gcol, grow, lay):
    b, n = lay.batch, lay.n
    L = MLSTM_CHUNK
    n_chunks, ctx_chunks = n // L, lay.n_ctx // L

    def cidx(d, j):
        return _scan_chunk_index(d == 1, j, n_chunks, ctx_chunks)

    return pl.pallas_call(
        _mlstm_scan_kernel,
        grid=(b, 2, n_chunks),
        in_specs=[
            pl.BlockSpec((1, MLSTM_QK, L), lambda bb, d, j: (bb, 0, cidx(d, j))),
            pl.BlockSpec((1, L, MLSTM_QK), lambda bb, d, j: (bb, cidx(d, j), 0)),
            pl.BlockSpec((1, D_MODEL, L), lambda bb, d, j: (bb, 0, cidx(d, j))),
            pl.BlockSpec((1, L, LANES), lambda bb, d, j: (bb, cidx(d, j), 0)),
            pl.BlockSpec((1, 4 * MLSTM_HEADS, L), lambda bb, d, j: (bb, 0, cidx(d, j))),
        ],
        out_specs=pl.BlockSpec((1, 1, D_MODEL, L), lambda bb, d, j: (d, bb, 0, cidx(d, j))),
        out_shape=jax.ShapeDtypeStruct((2, b, D_MODEL, n), F32),
        scratch_shapes=[
            pltpu.VMEM((MLSTM_HEADS, MLSTM_DV + LANES, MLSTM_DQK), F32),
            pltpu.VMEM((MLSTM_HEADS, SUBLANES, LANES), F32),
        ],
        compiler_params=_cparams(("parallel", "parallel", "arbitrary")),
        name="mlstm_scan",
    )(q_t, k, v_t, gcol, grow)


def _gqa_in_kernel(h_ref, g_ref, mod_ref, w_ref, qg_ref, kg_ref, cos_ref, sin_ref, q_ref, k_ref, v_ref):
    hd = ATTN_HEAD_DIM
    mod = mod_ref[0]
    u = _norm_mod(h_ref[0], g_ref[...], mod[0:1], mod[1:2]).astype(BF16)
    p = _dot(u, w_ref[...])
    cos = cos_ref[...]
    sin = sin_ref[...]
    lane = lax.broadcasted_iota(I32, cos.shape, 1)
    first_half = (lane & (hd // 4)) == 0

    def rope(xh):
        swapped = jnp.where(first_half, pltpu.roll(xh, hd - hd // 4, 1), pltpu.roll(xh, hd // 4, 1))
        return xh * cos + swapped * sin

    nq, nk = ATTN_HEADS * hd, ATTN_KV_HEADS * hd
    qn = _head_rms(p[:, :nq], hd)
    kn = _head_rms(p[:, nq:nq + nk], hd)
    qg = qg_ref[...]
    kg = kg_ref[...]
    for i in range(ATTN_HEADS):
        q_ref[0, :, i * hd:(i + 1) * hd] = (rope(qn[:, i * hd:(i + 1) * hd] * qg) * ATTN_Q_SCALE).astype(BF16)
    for i in range(ATTN_KV_HEADS):
        k_ref[0, :, i * hd:(i + 1) * hd] = rope(kn[:, i * hd:(i + 1) * hd] * kg).astype(BF16)
    v_ref[0] = p[:, nq + nk:].astype(BF16)


def _rope_tables(lay):
    hd = ATTN_HEAD_DIM
    quarter = hd // 4
    inv = ROPE_THETA ** (-jnp.arange(quarter, dtype=F32) / quarter)
    pos = jnp.arange(lay.n_lat)
    rows = (pos // GRID_W).astype(F32)
    cols = (pos % GRID_W).astype(F32)
    ang = jnp.concatenate([jnp.tile(rows[:, None] * inv, (1, 2)), jnp.tile(cols[:, None] * inv, (1, 2))], axis=1)
    sign = jnp.tile(jnp.concatenate([-jnp.ones(quarter, F32), jnp.ones(quarter, F32)]), 2)
    cos = jnp.concatenate([jnp.ones((lay.n_ctx, hd), F32), jnp.cos(ang)], axis=0)
    sin = jnp.concatenate([jnp.zeros((lay.n_ctx, hd), F32), jnp.sin(ang) * sign], axis=0)
    return cos, sin


def _gqa_in(h, norm_g, mod_l, w_in, q_g, k_g, lay):
    d, hd = D_MODEL, ATTN_HEAD_DIM
    nq, nk = ATTN_HEADS * hd, ATTN_KV_HEADS * hd
    cos, sin = _rope_tables(lay)
    b, n = lay.batch, lay.n
    tab_spec = pl.BlockSpec((TOKEN_TILE, hd), lambda bb, t: (t, 0))
    return pl.pallas_call(
        _gqa_in_kernel,
        grid=(b, lay.tiles),
        in_specs=[
            _tok_spec(d, lay), _full_spec((1, d)), _mod_spec(lay), _full_spec((d, nq + 2 * nk)),
            _full_spec((1, hd)), _full_spec((1, hd)), tab_spec, tab_spec,
        ],
        out_specs=[_tok_spec(nq, lay), _tok_spec(nk, lay), _tok_spec(nk, lay)],
        out_shape=[
            jax.ShapeDtypeStruct((b, n, nq), BF16),
            jax.ShapeDtypeStruct((b, n, nk), BF16),
            jax.ShapeDtypeStruct((b, n, nk), BF16),
        ],
        compiler_params=_cparams(("parallel", "parallel")),
        name="gqa_in",
    )(h, norm_g.reshape(1, d), mod_l, w_in.astype(BF16), q_g.reshape(1, hd), k_g.reshape(1, hd), cos, sin)


def _attn_kernel(q_ref, k_ref, v_ref, o_ref, *, ctx_tiles, n_ctx, n_all):
    hd = ATTN_HEAD_DIM
    t = pl.program_id(2)
    q = q_ref[0]
    rows = q.shape[0]
    q2 = jnp.concatenate([q[:, g * hd:(g + 1) * hd] for g in range(ATTN_GROUP)], axis=0)

    def attend(n_keys):
        bounds = [0, n_ctx] + list(range(n_ctx + ATTN_KV_CHUNK, n_keys + 1, ATTN_KV_CHUNK))
        assert bounds[-1] == n_keys
        m = l = acc = None
        for lo, hi in zip(bounds[:-1], bounds[1:]):
            s = _dot_nt(q2, k_ref[0, lo:hi, :])
            mc = jnp.max(s, axis=-1, keepdims=True)
            if m is None:
                m_new = mc
                p = jnp.exp2(s - m_new)
                l = jnp.sum(p, axis=-1, keepdims=True)
                acc = _dot(p.astype(BF16), v_ref[0, lo:hi, :])
            else:
                m_new = jnp.maximum(m, mc)
                alpha = jnp.exp2(m - m_new)
                p = jnp.exp2(s - m_new)
                l = alpha * l + jnp.sum(p, axis=-1, keepdims=True)
                acc = alpha * acc + _dot(p.astype(BF16), v_ref[0, lo:hi, :])
            m = m_new
        o = acc / l
        for g in range(ATTN_GROUP):
            o_ref[0, :, g * hd:(g + 1) * hd] = o[g * rows:(g + 1) * rows].astype(BF16)

    @pl.when(t < ctx_tiles)
    def _():
        attend(n_ctx)

    @pl.when(t >= ctx_tiles)
    def _():
        attend(n_all)


def _attention(q, k, v, lay):
    hd = ATTN_HEAD_DIM
    b, n = lay.batch, lay.n
    gw = ATTN_GROUP * hd
    kern = functools.partial(_attn_kernel, ctx_tiles=lay.ctx_tiles, n_ctx=lay.n_ctx, n_all=n)
    return pl.pallas_call(
        kern,
        grid=(b, ATTN_KV_HEADS, lay.tiles),
        in_specs=[
            pl.BlockSpec((1, TOKEN_TILE, gw), lambda bb, kh, t: (bb, t, kh)),
            pl.BlockSpec((1, n, hd), lambda bb, kh, t: (bb, 0, kh)),
            pl.BlockSpec((1, n, hd), lambda bb, kh, t: (bb, 0, kh)),
        ],
        out_specs=pl.BlockSpec((1, TOKEN_TILE, gw), lambda bb, kh, t: (bb, t, kh)),
        out_shape=jax.ShapeDtypeStruct((b, n, ATTN_HEADS * hd), BF16),
        compiler_params=_cparams(("parallel", "parallel", "parallel")),
        name="attention",
    )(q, k, v)


def _gla_in_kernel(h_ref, g_ref, mod_ref, w_ref, wa_ref, aw_ref, ab_ref,
                   q_ref, k_ref, v_ref, gate_ref, laf_ref, lab_ref):
    d = D_MODEL
    mod = mod_ref[0]
    u = _norm_mod(h_ref[0], g_ref[...], mod[0:1], mod[1:2]).astype(BF16)
    p = _dot(u, w_ref[...])
    q_ref[0] = p[:, :GLA_QK] * GLA_DK ** -0.5
    k_ref[0] = p[:, GLA_QK:2 * GLA_QK]
    v_ref[0] = p[:, 2 * GLA_QK:2 * GLA_QK + d].astype(BF16)
    gate_ref[0] = p[:, 2 * GLA_QK + d:]
    a_low = _dot(u, wa_ref[...]).astype(BF16)
    pre = _dot(a_low, aw_ref[...]) + ab_ref[...]
    la = _log_sigmoid(pre) * (1.0 / GLA_TAU)
    laf_ref[0] = la[:, :GLA_QK]
    lab_ref[0] = la[:, GLA_QK:]


def _gla_in(h, norm_g, mod_l, w_in, alpha_w2, alpha_b, lay):
    d, r = D_MODEL, GLA_GATE_RANK
    n_main = 2 * GLA_QK + 2 * d
    w_main = w_in[:, :n_main].astype(BF16)
    wa = jnp.zeros((d, LANES), F32).at[:, :2 * r].set(w_in[:, n_main:]).astype(BF16)
    aw = jnp.zeros((LANES, 2 * GLA_QK), F32)
    aw = aw.at[:r, :GLA_QK].set(alpha_w2[0]).at[r:2 * r, GLA_QK:].set(alpha_w2[1]).astype(BF16)
    ab = alpha_b.reshape(1, 2 * GLA_QK)
    b, n = lay.batch, lay.n
    return pl.pallas_call(
        _gla_in_kernel,
        grid=(b, lay.tiles),
        in_specs=[
            _tok_spec(d, lay), _full_spec((1, d)), _mod_spec(lay), _full_spec((d, n_main)),
            _full_spec((d, LANES)), _full_spec((LANES, 2 * GLA_QK)), _full_spec((1, 2 * GLA_QK)),
        ],
        out_specs=[_tok_spec(GLA_QK, lay), _tok_spec(GLA_QK, lay), _tok_spec(d, lay), _tok_spec(d, lay),
                   _tok_spec(GLA_QK, lay), _tok_spec(GLA_QK, lay)],
        out_shape=[
            jax.ShapeDtypeStruct((b, n, GLA_QK), F32),
            jax.ShapeDtypeStruct((b, n, GLA_QK), F32),
            jax.ShapeDtypeStruct((b, n, d), BF16),
            jax.ShapeDtypeStruct((b, n, d), F32),
            jax.ShapeDtypeStruct((b, n, GLA_QK), F32),
            jax.ShapeDtypeStruct((b, n, GLA_QK), F32),
        ],
        compiler_params=_cparams(("parallel", "parallel")),
        name="gla_in",
    )(h, norm_g.reshape(1, d), mod_l, w_main, wa, aw, ab)


def _gla_scan_kernel(q_ref, k_ref, v_ref, la_ref, out_ref, st_ref, *, reverse):
    T, C = GLA_TILE, GLA_CHUNK
    dk, dv, nh = GLA_DK, GLA_DV, GLA_HEADS
    j = pl.program_id(1)

    @pl.when(j == 0)
    def _():
        st_ref[...] = jnp.zeros_like(st_ref)

    n_chunks = T // C
    row = lax.broadcasted_iota(I32, (T, T), 0)
    col = lax.broadcasted_iota(I32, (T, T), 1)
    before = (col >= row) if reverse else (col <= row)
    b = jnp.dot(before.astype(F32), la_ref[0], precision=HIGHEST, preferred_element_type=F32)
    b_end = b[0:1] if reverse else b[T - 1:T]
    q = q_ref[0]
    k = k_ref[0]
    q_state = (q * jnp.exp(b)).astype(BF16)
    k_state = (k * jnp.exp(b_end - b)).astype(BF16)
    decay = jnp.exp(b_end)

    def rows_of(c):
        return slice(c * C, (c + 1) * C)

    b_mid = jnp.concatenate(
        [jnp.broadcast_to(b[c * C + C // 2:c * C + C // 2 + 1], (C, b.shape[1])) for c in range(n_chunks)], axis=0)
    q_mid = (q * jnp.exp(b - b_mid)).astype(BF16)
    k_mid = (k * jnp.exp(b_mid - b)).astype(BF16)
    same_chunk = (row // C) == (col // C)
    scan_order = list(reversed(range(n_chunks))) if reverse else list(range(n_chunks))
    later = scan_order[1:]
    q_off, k_off, off_mask = {}, {}, {}
    for c in later:
        edge = (c + 1) * C if reverse else c * C - 1
        b_edge = b[edge:edge + 1]
        q_off[c] = (q[rows_of(c)] * jnp.exp(b[rows_of(c)] - b_edge)).astype(BF16)
        k_off[c] = (k * jnp.exp(jnp.minimum(b_edge - b, 0.0))).astype(BF16)
        ccol = lax.broadcasted_iota(I32, (C, T), 1)
        off_mask[c] = (ccol >= (c + 1) * C) if reverse else (ccol < c * C)

    for h in range(nh):
        ks = slice(h * dk, (h + 1) * dk)
        vs = slice(h * dv, (h + 1) * dv)
        v_h = v_ref[0, :, vs]
        a_same = jnp.where(same_chunk & before, _dot_nt(q_mid[:, ks], k_mid[:, ks]), 0.0)
        blocks = []
        for c in range(n_chunks):
            a_c = a_same[rows_of(c)]
            if c in q_off:
                a_c = a_c + jnp.where(off_mask[c], _dot_nt(q_off[c][:, ks], k_off[c][:, ks]), 0.0)
            blocks.append(a_c)
        a = jnp.concatenate(blocks, axis=0).astype(BF16)
        st = st_ref[h]
        out_ref[0, :, vs] = _dot_nt(q_state[:, ks], st.astype(BF16)) + _dot(a, v_h)
        st_ref[h] = st * decay[:, ks] + _dot_tn(v_h, k_state[:, ks])


def _gla_scan(q, k, v, la, lay, reverse):
    b, n = lay.batch, lay.n
    n_tiles, ctx_tiles = n // GLA_TILE, lay.n_ctx // GLA_TILE

    def tidx(j):
        return _scan_chunk_index(reverse, j, n_tiles, ctx_tiles)

    kern = functools.partial(_gla_scan_kernel, reverse=reverse)
    return pl.pallas_call(
        kern,
        grid=(b, n_tiles),
        in_specs=[
            pl.BlockSpec((1, GLA_TILE, GLA_QK), lambda bb, j: (bb, tidx(j), 0)),
            pl.BlockSpec((1, GLA_TILE, GLA_QK), lambda bb, j: (bb, tidx(j), 0)),
            pl.BlockSpec((1, GLA_TILE, D_MODEL), lambda bb, j: (bb, tidx(j), 0)),
            pl.BlockSpec((1, GLA_TILE, GLA_QK), lambda bb, j: (bb, tidx(j), 0)),
        ],
        out_specs=pl.BlockSpec((1, GLA_TILE, D_MODEL), lambda bb, j: (bb, tidx(j), 0)),
        out_shape=jax.ShapeDtypeStruct((b, n, D_MODEL), F32),
        scratch_shapes=[pltpu.VMEM((GLA_HEADS, GLA_DV, GLA_DK), F32)],
        compiler_params=_cparams(("parallel", "arbitrary")),
        name="gla_scan_bwd" if reverse else "gla_scan_fwd",
    )(q, k, v, la)


def _readout_kernel(*refs, mode, first_routed_tile):
    if mode == "attn":
        a_ref, wout_ref = refs[:2]
        rest = refs[2:]
        y_in = a_ref[0]
    else:
        a_ref, b_ref, gsrc_ref, ng_ref, wout_ref = refs[:5]
        rest = refs[5:]
        hs = a_ref[0] + b_ref[0]
        if mode == "mlstm":
            normed = []
            for i in range(MLSTM_HEADS):
                xs = hs[i * MLSTM_DV:(i + 1) * MLSTM_DV, :]
                ms = jnp.mean(xs * xs, axis=0, keepdims=True)
                normed.append(xs * lax.rsqrt(ms + NORM_EPS))
            y_in = (jnp.concatenate(normed, axis=0) * ng_ref[...] * _sigmoid(gsrc_ref[0])).T
        else:
            gsrc = gsrc_ref[0]
            y_in = _head_rms(hs, GLA_DV) * ng_ref[...] * (gsrc * _sigmoid(gsrc))
        y_in = y_in.astype(BF16)
    (h_ref, mod_ref, n2g_ref, wrh_ref, wrl_ref, br_ref,
     hnew_ref, v_ref, e_ref, gate_ref, rank_ref, cnt_ref, carry_ref) = rest

    first = (pl.program_id(0) == 0) & (pl.program_id(1) == 0)

    @pl.when(first)
    def _():
        carry_ref[...] = jnp.zeros_like(carry_ref)

    mod = mod_ref[0]
    hn = h_ref[0] + mod[2:3] * _dot(y_in, wout_ref[...])
    hnew_ref[0] = hn
    v = _norm_mod(hn, n2g_ref[...], mod[3:4], mod[4:5])
    for c in range(ROW_TILE):
        v_ref[0, pl.ds(c, v.shape[0], stride=ROW_TILE), :] = v[:, c * LANES:(c + 1) * LANES]

    v_hi = v.astype(BF16)
    v_lo = (v - v_hi.astype(F32)).astype(BF16)
    w_hi = wrh_ref[...]
    logits = _dot(v_hi, w_hi) + _dot(v_lo, w_hi) + _dot(v_hi, wrl_ref[...]) + br_ref[...]

    tm = logits.shape[0]
    lane = lax.broadcasted_iota(I32, logits.shape, 1)
    lane_f = lane.astype(F32)
    work = logits
    top_e = jnp.zeros(logits.shape, F32)
    top_p = jnp.zeros(logits.shape, F32)
    onehot = jnp.zeros(logits.shape, F32)
    m0 = None
    hits = []
    for jx in range(TOP_K):
        mx = jnp.max(work, axis=-1, keepdims=True)
        idx = jnp.min(jnp.where(work == mx, lane_f, float(LANES)), axis=-1, keepdims=True)
        if jx == 0:
            m0 = mx
        hit = lane_f == idx
        hits.append(hit)
        top_e = jnp.where(lane == jx, idx, top_e)
        top_p = jnp.where(lane == jx, jnp.exp(mx - m0), top_p)
        onehot = jnp.where(hit, 1.0, onehot)
        work = jnp.where(hit, -jnp.inf, work)
    e_ref[0] = top_e.astype(I32)
    gate_ref[0] = top_p / jnp.sum(top_p, axis=-1, keepdims=True)
    if first_routed_tile:
        onehot = onehot * (pl.program_id(1) >= first_routed_tile).astype(F32)

    r = lax.broadcasted_iota(I32, (tm, tm), 0)
    c = lax.broadcasted_iota(I32, (tm, tm), 1)
    tril = (c <= r).astype(BF16)
    cum = _dot(tril, onehot.astype(BF16))
    carry = carry_ref[0:1, :]
    rank_all = carry + cum - onehot
    rank_sel = jnp.zeros(logits.shape, F32)
    for jx in range(TOP_K):
        picked = jnp.sum(jnp.where(hits[jx], rank_all, 0.0), axis=-1, keepdims=True)
        rank_sel = jnp.where(lane == jx, picked, rank_sel)
    rank_ref[0] = rank_sel.astype(I32)
    total = carry + cum[tm - 1:tm, :]
    carry_ref[...] = jnp.broadcast_to(total, carry_ref.shape)
    cnt_ref[...] = jnp.broadcast_to(total, cnt_ref.shape)


def _readout(mode, mixer_outs, w_out, h, mod_l, norm2_g, router_w, router_b, lay, latent_only):
    d = D_MODEL
    b, n = lay.batch, lay.n
    wr = jnp.zeros((d, LANES), F32).at[:, :N_EXPERTS].set(router_w)
    wr_hi = wr.astype(BF16)
    wr_lo = (wr - wr_hi.astype(F32)).astype(BF16)
    br = jnp.full((1, LANES), NEG_BIG, F32).at[0, :N_EXPERTS].set(router_b)
    if mode == "attn":
        (attn_o,) = mixer_outs
        head_in = [attn_o]
        head_specs = [_tok_spec(d, lay)]
    elif mode == "mlstm":
        h_dir, o_gate, norm_g = mixer_outs
        head_in = [h_dir, h_dir, o_gate, norm_g.reshape(d, 1)]
        head_specs = [pl.BlockSpec((None, 1, d, TOKEN_TILE), lambda bb, t: (0, bb, 0, t)),
                      pl.BlockSpec((None, 1, d, TOKEN_TILE), lambda bb, t: (1, bb, 0, t)),
                      _feat_spec(d), _full_spec((d, 1))]
    else:
        o_f, o_b, g_gate, norm_g = mixer_outs
        head_in = [o_f, o_b, g_gate, norm_g.reshape(1, d)]
        head_specs = [_tok_spec(d, lay), _tok_spec(d, lay), _tok_spec(d, lay), _full_spec((1, d))]
    kern = functools.partial(_readout_kernel, mode=mode, first_routed_tile=lay.ctx_tiles if latent_only else 0)
    return pl.pallas_call(
        kern,
        grid=(b, lay.tiles),
        in_specs=head_specs + [
            _full_spec((d, d)), _tok_spec(d, lay), _mod_spec(lay), _full_spec((1, d)),
            _full_spec((d, LANES)), _full_spec((d, LANES)), _full_spec((1, LANES)),
        ],
        out_specs=[_tok_spec(d, lay), pl.BlockSpec((1, TOKEN_TILE * ROW_TILE, LANES), lambda bb, t: (bb, t, 0)),
                   _tok_spec(LANES, lay), _tok_spec(LANES, lay),
                   _tok_spec(LANES, lay), _full_spec((SUBLANES, LANES))],
        out_shape=[
            jax.ShapeDtypeStruct((b, n, d), F32),
            jax.ShapeDtypeStruct((b, n * ROW_TILE, LANES), F32),
            jax.ShapeDtypeStruct((b, n, LANES), I32),
            jax.ShapeDtypeStruct((b, n, LANES), F32),
            jax.ShapeDtypeStruct((b, n, LANES), I32),
            jax.ShapeDtypeStruct((SUBLANES, LANES), F32),
        ],
        scratch_shapes=[pltpu.VMEM((SUBLANES, LANES), F32)],
        compiler_params=_cparams(("arbitrary", "arbitrary")),
        name="readout_" + mode,
    )(*head_in, w_out.astype(BF16), h, mod_l, norm2_g.reshape(1, d), wr_hi, wr_lo, br)


def _slot(ps_ref, e_ref, r_ref, i):
    return ps_ref[e_ref[0, 0, i]] + r_ref[0, 0, i]


def _row_copies_wait(src_block, dst_rows, sem):
    for _ in range(TOP_K):
        pltpu.make_async_copy(src_block, dst_rows, sem).wait()


def _zero_fill(zs_ref, nu_ref, xb_ref, zbuf_ref, sem):
    zbuf_ref[...] = jnp.zeros_like(zbuf_ref)
    blk_rows = zbuf_ref.shape[0]
    n_blocks = xb_ref.shape[0] // blk_rows

    def zero_block(start):
        return pltpu.make_async_copy(zbuf_ref, xb_ref.at[pl.ds(start, blk_rows), :], sem)

    def trailing(i, then):
        pl.when(i >= nu_ref[0])(lambda: then(zero_block(i * blk_rows)))

    for e in range(N_EXPERTS):
        zero_block(pl.multiple_of(zs_ref[e] * ROW_TILE, ROW_TILE)).start()
    for e in range(N_EXPERTS):
        zero_block(0).wait()
    for i in range(n_blocks - N_EXPERTS, n_blocks):
        trailing(i, lambda cp: cp.start())
    for i in range(n_blocks - N_EXPERTS, n_blocks):
        trailing(i, lambda cp: cp.wait())


def _dispatch_kernel(e_ref, r_ref, ps_ref, zs_ref, nu_ref, v_ref, xb_ref, zbuf_ref, sem, *, first_routed_tile):
    rows = v_ref.shape[1] // ROW_TILE

    @pl.when((pl.program_id(0) == 0) & (pl.program_id(1) == 0))
    def _():
        _zero_fill(zs_ref, nu_ref, xb_ref, zbuf_ref, sem)

    def body(r, carry):
        src = v_ref.at[0, pl.ds(pl.multiple_of(r * ROW_TILE, ROW_TILE), ROW_TILE), :]
        for jx in range(TOP_K):
            slot = _slot(ps_ref, e_ref, r_ref, r * TOP_K + jx)
            dst = xb_ref.at[pl.ds(pl.multiple_of(slot * ROW_TILE, ROW_TILE), ROW_TILE), :]
            pltpu.make_async_copy(src, dst, sem).start(priority=jx % DMA_THREADS)
        return carry

    @pl.when(pl.program_id(1) >= first_routed_tile)
    def _():
        lax.fori_loop(0, rows, body, 0, unroll=DMA_UNROLL)
        _row_copies_wait(v_ref.at[0], xb_ref.at[pl.ds(0, rows * ROW_TILE), :], sem)


def _assign_spec(lay):
    tiles = lay.tiles
    return pl.BlockSpec((1, 1, TOKEN_TILE * TOP_K), lambda bb, t: (bb * tiles + t, 0, 0), memory_space=pltpu.SMEM)


def _dispatch(e_c, r_c, pad_starts, zero_start, n_used, v_rows, cap, lay, latent_only):
    return pl.pallas_call(
        functools.partial(_dispatch_kernel, first_routed_tile=lay.ctx_tiles if latent_only else 0),
        grid=(lay.batch, lay.tiles),
        in_specs=[
            _assign_spec(lay), _assign_spec(lay),
            pl.BlockSpec(memory_space=pltpu.SMEM),
            pl.BlockSpec(memory_space=pltpu.SMEM),
            pl.BlockSpec(memory_space=pltpu.SMEM),
            pl.BlockSpec((1, TOKEN_TILE * ROW_TILE, LANES), lambda bb, t: (bb, t, 0)),
        ],
        out_specs=pl.BlockSpec(memory_space=pl.ANY),
        out_shape=jax.ShapeDtypeStruct((cap * ROW_TILE, LANES), F32),
        scratch_shapes=[pltpu.VMEM((MOE_BLOCK * ROW_TILE, LANES), F32), pltpu.SemaphoreType.DMA(())],
        compiler_params=_cparams(("arbitrary", "arbitrary")),
        name="moe_dispatch",
    )(e_c, r_c, pad_starts, zero_start, n_used, v_rows)


def _expert_kernel(be_ref, nu_ref, x_ref, wgu_ref, bgu_ref, wd_ref, bd_ref, y_ref, wgu_bf_ref, wd_bf_ref):
    i = pl.program_id(0)
    e = be_ref[i]
    used = i < nu_ref[0]
    new_expert = (i == 0) | (e != be_ref[jnp.maximum(i - 1, 0)])

    @pl.when(new_expert)
    def _():
        wgu_bf_ref[...] = wgu_ref[0, 0].astype(BF16)
        wd_bf_ref[...] = wd_ref[0, 0].astype(BF16)

    @pl.when(jnp.logical_not(used))
    def _():
        y_ref[...] = jnp.zeros_like(y_ref)

    @pl.when(used)
    def _():
        x = jnp.concatenate([x_ref[pl.ds(c, MOE_BLOCK, stride=ROW_TILE), :] for c in range(ROW_TILE)], axis=1)
        gu = _dot(x.astype(BF16), wgu_bf_ref[...]) + bgu_ref[0, 0]
        g = jnp.minimum(gu[:, :D_EXPERT], SWIGLU_LIMIT)
        lin = jnp.clip(gu[:, D_EXPERT:], -SWIGLU_LIMIT, SWIGLU_LIMIT)
        hdn = g * _sigmoid(SWIGLU_ALPHA * g) * (lin + 1.0)
        y = _dot(hdn.astype(BF16), wd_bf_ref[...]) + bd_ref[0, 0]
        for c in range(ROW_TILE):
            y_ref[pl.ds(c, MOE_BLOCK, stride=ROW_TILE), :] = y[:, c * LANES:(c + 1) * LANES]


def _experts(block_e, n_used, xb, w_gu, b_gu, w_d, b_d, layer):
    d, de = D_MODEL, D_EXPERT
    depth = w_gu.shape[0]
    blk_rows = MOE_BLOCK * ROW_TILE
    n_blocks = xb.shape[0] // blk_rows

    def row_map(i, be, nu):
        return (jnp.minimum(i, nu[0] - 1), 0)

    def exp_map(i, be, nu):
        return (layer, be[i], 0, 0)

    grid_spec = pltpu.PrefetchScalarGridSpec(
        num_scalar_prefetch=2,
        grid=(n_blocks,),
        in_specs=[
            pl.BlockSpec((blk_rows, LANES), row_map),
            pl.BlockSpec((1, 1, d, 2 * de), exp_map),
            pl.BlockSpec((1, 1, 1, 2 * de), exp_map),
            pl.BlockSpec((1, 1, de, d), exp_map),
            pl.BlockSpec((1, 1, 1, d), exp_map),
        ],
        out_specs=pl.BlockSpec((blk_rows, LANES), lambda i, be, nu: (i, 0)),
        scratch_shapes=[pltpu.VMEM((d, 2 * de), BF16), pltpu.VMEM((de, d), BF16)],
    )
    return pl.pallas_call(
        _expert_kernel,
        grid_spec=grid_spec,
        out_shape=jax.ShapeDtypeStruct(xb.shape, F32),
        compiler_params=_cparams(("arbitrary",), vmem=EXPERT_VMEM_LIMIT),
        name="moe_experts",
    )(block_e, n_used, xb, w_gu, b_gu.reshape(depth, N_EXPERTS, 1, 2 * de), w_d,
      b_d.reshape(depth, N_EXPERTS, 1, d))


def _combine_kernel(e_ref, r_ref, en_ref, rn_ref, gate_ref, ps_ref, yb_ref, h_ref, mod_ref, out_ref,
                    ybuf_ref, acc_ref, sems, *, tiles_per_sample, n_tiles):
    rows = h_ref.shape[1]
    tile = pl.program_id(0) * tiles_per_sample + pl.program_id(1)
    buf = tile % 2

    def issue_rows(e_src, r_src, into):
        def issue(r, carry):
            base = pl.multiple_of(r * ROW_TILE, ROW_TILE)
            for jx in range(TOP_K):
                slot = _slot(ps_ref, e_src, r_src, r * TOP_K + jx)
                src = yb_ref.at[pl.ds(pl.multiple_of(slot * ROW_TILE, ROW_TILE), ROW_TILE), :]
                pltpu.make_async_copy(src, ybuf_ref.at[into, jx, pl.ds(base, ROW_TILE), :], sems.at[into]).start(
                    priority=jx % DMA_THREADS)
            return carry

        lax.fori_loop(0, rows, issue, 0, unroll=DMA_UNROLL)

    @pl.when(tile == 0)
    def _():
        issue_rows(e_ref, r_ref, 0)

    @pl.when(tile + 1 < n_tiles)
    def _():
        issue_rows(en_ref, rn_ref, 1 - buf)

    _row_copies_wait(yb_ref.at[pl.ds(0, rows * ROW_TILE), :], ybuf_ref.at[buf, 0], sems.at[buf])

    def accumulate(r, carry):
        base = pl.multiple_of(r * ROW_TILE, ROW_TILE)
        a = gate_ref[0, 0, r * TOP_K] * ybuf_ref[buf, 0, pl.ds(base, ROW_TILE), :]
        for jx in range(1, TOP_K):
            a = a + gate_ref[0, 0, r * TOP_K + jx] * ybuf_ref[buf, jx, pl.ds(base, ROW_TILE), :]
        acc_ref[pl.ds(base, ROW_TILE), :] = a
        return carry

    lax.fori_loop(0, rows, accumulate, 0, unroll=DMA_UNROLL)
    for c in range(ROW_TILE):
        cs = slice(c * LANES, (c + 1) * LANES)
        out_ref[0, :, cs] = h_ref[0, :, cs] + mod_ref[0, 5:6, cs] * acc_ref[pl.ds(c, rows, stride=ROW_TILE), :]


def _combine(e_c, r_c, gate_c, pad_starts, yb, h, mod_l, lay, latent_only):
    d = D_MODEL
    tiles = lay.tiles
    n_tiles = lay.batch * tiles
    if latent_only:
        out_spec = pl.BlockSpec((1, TOKEN_TILE, d), lambda bb, t: (bb, jnp.maximum(t - lay.ctx_tiles, 0), 0))
        out_rows = lay.n_lat
    else:
        out_spec = _tok_spec(d, lay)
        out_rows = lay.n
    next_spec = pl.BlockSpec((1, 1, TOKEN_TILE * TOP_K),
                             lambda bb, t: (jnp.minimum(bb * tiles + t + 1, n_tiles - 1), 0, 0),
                             memory_space=pltpu.SMEM)
    return pl.pallas_call(
        functools.partial(_combine_kernel, tiles_per_sample=tiles, n_tiles=n_tiles),
        grid=(lay.batch, tiles),
        in_specs=[_assign_spec(lay), _assign_spec(lay), next_spec, next_spec, _assign_spec(lay),
                  pl.BlockSpec(memory_space=pltpu.SMEM),
                  pl.BlockSpec(memory_space=pl.ANY), _tok_spec(d, lay), _mod_spec(lay)],
        out_specs=out_spec,
        out_shape=jax.ShapeDtypeStruct((lay.batch, out_rows, d), F32),
        scratch_shapes=[pltpu.VMEM((2, TOP_K, TOKEN_TILE * ROW_TILE, LANES), F32),
                        pltpu.VMEM((TOKEN_TILE * ROW_TILE, LANES), F32), pltpu.SemaphoreType.DMA((2,))],
        compiler_params=_cparams(("arbitrary", "arbitrary")),
        name="moe_combine",
    )(e_c, r_c, e_c, r_c, gate_c, pad_starts, yb, h, mod_l)


def _moe(h, v_rows, top_e, gate, rank, counts, mod_l, w_gu, b_gu, w_d, b_d, layer, lay, latent_only):
    n_tokens = lay.batch * lay.n
    n_assign = n_tokens * TOP_K
    cap = (-(-n_assign // MOE_BLOCK) + N_EXPERTS) * MOE_BLOCK
    n_blocks = cap // MOE_BLOCK
    cnt = counts[0, :N_EXPERTS].astype(I32)
    padded = (cnt + MOE_BLOCK - 1) // MOE_BLOCK * MOE_BLOCK
    pad_ends = jnp.cumsum(padded)
    pad_starts = pad_ends - padded
    n_used = (pad_ends[-1] // MOE_BLOCK).astype(I32).reshape(1)
    blk = jnp.minimum(jnp.arange(n_blocks, dtype=I32), n_used[0] - 1) * MOE_BLOCK
    block_e = jnp.minimum(jnp.sum((pad_ends[None, :] <= blk[:, None]).astype(I32), axis=1), N_EXPERTS - 1)
    pad_starts = pad_starts.astype(I32)
    zero_start = jnp.minimum(pad_starts + cnt, cap - MOE_BLOCK).astype(I32)

    def per_assignment(a):
        return a[:, :, :TOP_K].reshape(lay.batch * lay.tiles, 1, TOKEN_TILE * TOP_K)

    e_c, r_c, gate_c = per_assignment(top_e), per_assignment(rank), per_assignment(gate)
    xb = _dispatch(e_c, r_c, pad_starts, zero_start, n_used, v_rows, cap, lay, latent_only)
    yb = _experts(block_e, n_used, xb, w_gu, b_gu, w_d, b_d, layer)
    return _combine(e_c, r_c, gate_c, pad_starts, yb, h, mod_l, lay, latent_only)


def kernel(x, c, ctx, c_ctx, norm1_g, norm2_g, mod_w, mod_b,
           mlstm_w_in, mlstm_conv_w, mlstm_gate_b, mlstm_out_norm_g, mlstm_w_out,
           attn_w_in, attn_q_norm_g, attn_k_norm_g, attn_w_out,
           gla_w_in, gla_alpha_w2, gla_alpha_b, gla_out_norm_g, gla_w_out,
           router_w, router_b, moe_w_gu, moe_b_gu, moe_w_down, moe_b_down):
    batch, n_lat, d = x.shape
    n_ctx = ctx.shape[1]
    depth = norm1_g.shape[0]
    assert d == D_MODEL
    lay = _Layout(batch, n_ctx, n_lat)

    cond_rows = -(-(batch + 1) // SUBLANES) * SUBLANES
    cond = jnp.zeros((cond_rows, d), F32).at[:batch].set(c).at[batch].set(c_ctx)
    mod_all = _modulation(cond, mod_w, mod_b)

    h = jnp.concatenate([ctx, x], axis=1)
    for layer in range(depth):
        kind, j = layer % 3, layer // 3
        last = layer == depth - 1
        mod_l = mod_all[layer]
        if kind == 0:
            qk_pre, v_t, o_gate, gcol, grow = _mlstm_in(h, norm1_g[layer], mod_l, mlstm_w_in[j],
                                                        mlstm_gate_b[j], lay)
            q_t, k_m = _mlstm_conv(qk_pre, mlstm_conv_w[j], lay)
            h_dir = _mlstm_scan(q_t, k_m, v_t, gcol, grow, lay)
            mixer = ("mlstm", (h_dir, o_gate, mlstm_out_norm_g[j]), mlstm_w_out[j])
        elif kind == 1:
            q_a, k_a, v_a = _gqa_in(h, norm1_g[layer], mod_l, attn_w_in[j], attn_q_norm_g[j],
                                    attn_k_norm_g[j], lay)
            mixer = ("attn", (_attention(q_a, k_a, v_a, lay),), attn_w_out[j])
        else:
            q_g, k_g, v_g, g_gate, la_f, la_b = _gla_in(h, norm1_g[layer], mod_l, gla_w_in[j],
                                                        gla_alpha_w2[j], gla_alpha_b[j], lay)
            o_f = _gla_scan(q_g, k_g, v_g, la_f, lay, reverse=False)
            o_b = _gla_scan(q_g, k_g, v_g, la_b, lay, reverse=True)
            mixer = ("gla", (o_f, o_b, g_gate, gla_out_norm_g[j]), gla_w_out[j])
        mode, mixer_outs, w_out = mixer
        h, v_moe, top_e, gate, rank, counts = _readout(mode, mixer_outs, w_out, h, mod_l, norm2_g[layer],
                                                       router_w[layer], router_b[layer], lay, last)
        h = _moe(h, v_moe, top_e, gate, rank, counts, mod_l,
                 moe_w_gu, moe_b_gu, moe_w_down, moe_b_down, layer, lay, latent_only=last)
    return h
```

```python
import functools

import jax
import jax.numpy as jnp
from jax import lax
from jax.experimental import pallas as pl
from jax.experimental.pallas import tpu as pltpu

F32 = jnp.float32
BF16 = jnp.bfloat16
I32 = jnp.int32
HIGHEST = lax.Precision.HIGHEST

D_MODEL = 1024
GRID_W = 64
NORM_EPS = 1e-6

MLSTM_HEADS = 8
MLSTM_DQK = 64
MLSTM_DV = 128
MLSTM_QK = MLSTM_HEADS * MLSTM_DQK

ATTN_HEADS = 8
ATTN_KV_HEADS = 4
ATTN_GROUP = ATTN_HEADS // ATTN_KV_HEADS
ATTN_HEAD_DIM = 128
ROPE_THETA = 10000.0

GLA_HEADS = 4
GLA_DK = 128
GLA_DV = 256
GLA_GATE_RANK = 16
GLA_TAU = 16.0
GLA_QK = GLA_HEADS * GLA_DK

N_EXPERTS = 32
TOP_K = 4
D_EXPERT = D_MODEL
SWIGLU_LIMIT = 7.0
SWIGLU_ALPHA = 1.702

LANES = 128
SUBLANES = 8
TOKEN_TILE = 256
MLSTM_CHUNK = 256
GLA_TILE = 128
GLA_CHUNK = 32
ATTN_KV_CHUNK = 256
ATTN_Q_SCALE = ATTN_HEAD_DIM ** -0.5 * 1.4426950408889634
MOE_BLOCK = 512
ROW_TILE = D_MODEL // LANES
assert ROW_TILE == SUBLANES
DMA_UNROLL = 8
DMA_THREADS = 2
VMEM_LIMIT = 48 * 1024 * 1024
EXPERT_VMEM_LIMIT = 56 * 1024 * 1024
NEG_BIG = -1e30


def _cparams(semantics, vmem=VMEM_LIMIT):
    return pltpu.CompilerParams(dimension_semantics=semantics, vmem_limit_bytes=vmem)


def _log_sigmoid(x):
    return jnp.minimum(x, 0.0) - jnp.log(1.0 + jnp.exp(-jnp.abs(x)))


def _sigmoid(x):
    return 1.0 / (1.0 + jnp.exp(-x))


def _norm_mod(x, g, shift, scale):
    ms = jnp.mean(x * x, axis=-1, keepdims=True)
    y = x * lax.rsqrt(ms + NORM_EPS) * g
    return y * (1.0 + scale) + shift


def _head_rms(x, head_dim):
    outs = []
    for i in range(x.shape[1] // head_dim):
        xs = x[:, i * head_dim:(i + 1) * head_dim]
        ms = jnp.mean(xs * xs, axis=-1, keepdims=True)
        outs.append(xs * lax.rsqrt(ms + NORM_EPS))
    return jnp.concatenate(outs, axis=1)


def _dot(a, b):
    return jnp.dot(a, b, preferred_element_type=F32)


def _dot_nt(a, b):
    return lax.dot_general(a, b, (((1,), (1,)), ((), ())), preferred_element_type=F32)


def _dot_tn(a, b):
    return lax.dot_general(a, b, (((0,), (0,)), ((), ())), preferred_element_type=F32)


def _mod_kernel(c_ref, w_ref, b_ref, o_ref):
    x = c_ref[...]
    x = x * _sigmoid(x)
    o_ref[0, 0] = _dot(x.astype(BF16), w_ref[0].astype(BF16)) + b_ref[0, 0]


def _modulation(cond_rows, mod_w, mod_b):
    depth, d, _ = mod_w.shape
    rows = cond_rows.shape[0]
    out = pl.pallas_call(
        _mod_kernel,
        grid=(depth, 6),
        in_specs=[
            pl.BlockSpec((rows, d), lambda l, j: (0, 0)),
            pl.BlockSpec((1, d, d), lambda l, j: (l, 0, j)),
            pl.BlockSpec((1, 1, 1, d), lambda l, j: (l, j, 0, 0)),
        ],
        out_specs=pl.BlockSpec((1, 1, rows, d), lambda l, j: (l, j, 0, 0)),
        out_shape=jax.ShapeDtypeStruct((depth, 6, rows, d), F32),
        compiler_params=_cparams(("parallel", "parallel")),
        name="modulation",
    )(cond_rows, mod_w, mod_b.reshape(depth, 6, 1, d))
    return out.transpose(0, 2, 1, 3)


class _Layout:
    def __init__(self, batch, n_ctx, n_lat):
        self.batch = batch
        self.n_ctx = n_ctx
        self.n_lat = n_lat
        self.n = n_ctx + n_lat
        assert n_ctx % TOKEN_TILE == 0 and n_lat % TOKEN_TILE == 0
        assert n_ctx % MLSTM_CHUNK == 0 and n_lat % MLSTM_CHUNK == 0
        assert n_ctx % GLA_TILE == 0 and n_lat % GLA_TILE == 0
        self.tiles = self.n // TOKEN_TILE
        self.ctx_tiles = n_ctx // TOKEN_TILE
        self.ctx_row = batch

    def mod_index(self, b, t):
        return jnp.where(t < self.ctx_tiles, self.ctx_row, b)


def _tok_spec(width, lay):
    return pl.BlockSpec((1, TOKEN_TILE, width), lambda b, t: (b, t, 0))


def _feat_spec(features):
    return pl.BlockSpec((1, features, TOKEN_TILE), lambda b, t: (b, 0, t))


def _full_spec(shape):
    nd = len(shape)
    return pl.BlockSpec(shape, lambda b, t: (0,) * nd)


def _mod_spec(lay):
    return pl.BlockSpec((1, 6, D_MODEL), lambda b, t: (lay.mod_index(b, t), 0, 0))


def _mlstm_in_kernel(h_ref, g_ref, mod_ref, w_ref, wvot_ref, wg_ref, wgt_ref, gb_ref, gbt_ref,
                     qk_ref, vt_ref, ot_ref, gc_ref, gr_ref):
    mod = mod_ref[0]
    u = _norm_mod(h_ref[0], g_ref[...], mod[0:1], mod[1:2]).astype(BF16)
    qk_ref[0] = _dot(u, w_ref[...])
    vo_t = _dot_nt(wvot_ref[...], u)
    vt_ref[0] = vo_t[:D_MODEL].astype(BF16)
    ot_ref[0] = vo_t[D_MODEL:]
    gc = _dot(u, wg_ref[...]) + gb_ref[...]
    lane = lax.broadcasted_iota(I32, gc.shape, 1)
    gc_ref[0] = jnp.where(((lane >> 3) & 1) == 1, _log_sigmoid(gc), gc)
    gr = _dot_nt(wgt_ref[...], u) + gbt_ref[...]
    sub = lax.broadcasted_iota(I32, gr.shape, 0)
    gr_ref[0] = jnp.where(((sub >> 3) & 1) == 1, _log_sigmoid(gr), gr)


def _mlstm_in(h, norm_g, mod_l, w_in, gate_b, lay):
    d = D_MODEL
    n_qk = 2 * MLSTM_QK
    n_main = n_qk + 2 * d
    n_gate = 4 * MLSTM_HEADS
    w_qk = w_in[:, :n_qk].astype(BF16)
    w_vo_t = w_in[:, n_qk:n_main].T.astype(BF16)
    w_gate = w_in[:, n_main:]
    wg = jnp.zeros((d, LANES), F32).at[:, :n_gate].set(w_gate).astype(BF16)
    wgt = w_gate.T.astype(BF16)
    gb = jnp.zeros((1, LANES), F32).at[0, :n_gate].set(gate_b.reshape(-1))
    gbt = gate_b.reshape(n_gate, 1)
    b, n = lay.batch, lay.n
    return pl.pallas_call(
        _mlstm_in_kernel,
        grid=(b, lay.tiles),
        in_specs=[
            _tok_spec(d, lay), _full_spec((1, d)), _mod_spec(lay),
            _full_spec((d, n_qk)), _full_spec((2 * d, d)), _full_spec((d, LANES)), _full_spec((n_gate, d)),
            _full_spec((1, LANES)), _full_spec((n_gate, 1)),
        ],
        out_specs=[
            _tok_spec(n_qk, lay), _feat_spec(d), _feat_spec(d), _tok_spec(LANES, lay), _feat_spec(n_gate),
        ],
        out_shape=[
            jax.ShapeDtypeStruct((b, n, n_qk), F32),
            jax.ShapeDtypeStruct((b, d, n), BF16),
            jax.ShapeDtypeStruct((b, d, n), F32),
            jax.ShapeDtypeStruct((b, n, LANES), F32),
            jax.ShapeDtypeStruct((b, n_gate, n), F32),
        ],
        compiler_params=_cparams(("parallel", "parallel")),
        name="mlstm_in",
    )(h, norm_g.reshape(1, d), mod_l, w_qk, w_vo_t, wg, wgt, gb, gbt)


def _mlstm_conv_kernel(x_ref, xp_ref, xn_ref, cw_ref, qt_ref, k_ref, *, tiles, ctx_tiles):
    t = pl.program_id(1)
    x = x_ref[0]
    rows = x.shape[0]
    seg_first = (t == 0) | (t == ctx_tiles)
    seg_last = (t == ctx_tiles - 1) | (t == tiles - 1)
    prev_row = jnp.where(seg_first, 0.0, xp_ref[0, SUBLANES - 1:SUBLANES, :])
    next_row = jnp.where(seg_last, 0.0, xn_ref[0, 0:1, :])
    rid = lax.broadcasted_iota(I32, x.shape, 0)
    x_m1 = jnp.where(rid == 0, prev_row, pltpu.roll(x, 1, 0))
    x_p1 = jnp.where(rid == rows - 1, next_row, pltpu.roll(x, rows - 1, 0))
    cw = cw_ref[...]
    y = cw[0:1] * x_m1 + cw[1:2] * x + cw[2:3] * x_p1
    qk = y * _sigmoid(y)
    qt_ref[0] = (qk[:, :MLSTM_QK] * MLSTM_DQK ** -0.5).T.astype(BF16)
    k_ref[0] = qk[:, MLSTM_QK:].astype(BF16)


def _mlstm_conv(qk_pre, conv_w, lay):
    b, n = lay.batch, lay.n
    n_qk = 2 * MLSTM_QK
    halo = TOKEN_TILE // SUBLANES
    n_halo = n // SUBLANES
    kern = functools.partial(_mlstm_conv_kernel, tiles=lay.tiles, ctx_tiles=lay.ctx_tiles)
    return pl.pallas_call(
        kern,
        grid=(b, lay.tiles),
        in_specs=[
            _tok_spec(n_qk, lay),
            pl.BlockSpec((1, SUBLANES, n_qk), lambda bb, t: (bb, jnp.maximum(t * halo - 1, 0), 0)),
            pl.BlockSpec((1, SUBLANES, n_qk), lambda bb, t: (bb, jnp.minimum((t + 1) * halo, n_halo - 1), 0)),
            _full_spec((3, n_qk)),
        ],
        out_specs=[_feat_spec(MLSTM_QK), _tok_spec(MLSTM_QK, lay)],
        out_shape=[
            jax.ShapeDtypeStruct((b, MLSTM_QK, n), BF16),
            jax.ShapeDtypeStruct((b, n, MLSTM_QK), BF16),
        ],
        compiler_params=_cparams(("parallel", "parallel")),
        name="mlstm_conv",
    )(qk_pre, qk_pre, qk_pre, conv_w)


def _scan_chunk_index(reverse, j, n_chunks, ctx_chunks):
    back = jnp.where(j < ctx_chunks, ctx_chunks - 1 - j, n_chunks - 1 - (j - ctx_chunks))
    return jnp.where(reverse, back, j)


def _mlstm_scan_kernel(qt_ref, k_ref, vt_ref, gc_ref, gr_ref, out_ref, st_ref, m_ref):
    L = MLSTM_CHUNK
    dk, dv, nh = MLSTM_DQK, MLSTM_DV, MLSTM_HEADS
    d = pl.program_id(1)
    j = pl.program_id(2)

    @pl.when(j == 0)
    def _():
        st_ref[...] = jnp.zeros_like(st_ref)
        m_ref[...] = jnp.zeros_like(m_ref)

    q_t = qt_ref[0]
    k = k_ref[0]
    v_t = vt_ref[0]

    gcol = gc_ref[0]
    gcol = jnp.where(d == 0, gcol, pltpu.roll(gcol, LANES - 2 * nh, 1))
    grow = gr_ref[0, pl.ds(pl.multiple_of(d * 2 * nh, 2 * nh), 2 * nh), :]

    row = lax.broadcasted_iota(I32, (L, L), 0)
    col = lax.broadcasted_iota(I32, (L, L), 1)
    sgn = 1 - 2 * d
    mask_t = sgn * (col - row) >= 0
    cum_rows = jnp.dot(grow, mask_t.astype(F32), precision=HIGHEST, preferred_element_type=F32)
    cum_cols = jnp.dot((sgn * (row - col) >= 0).astype(F32), gcol, precision=HIGHEST,
                       preferred_element_type=F32)
    b_minus_i = cum_cols - pltpu.roll(gcol, nh, 1)
    b_tot = jnp.sum(grow[nh:], axis=-1, keepdims=True)

    ones_row = (lax.broadcasted_iota(I32, (LANES, L), 0) == 0).astype(BF16)
    for h in range(nh):
        b_t = cum_rows[nh + h:nh + h + 1, :]
        li = grow[h:h + 1, :]
        m_prev = m_ref[h, 0:1, 0:1]
        dmat = jnp.where(mask_t, b_t - b_minus_i[:, nh + h:nh + h + 1], -jnp.inf)
        inter = b_t + m_prev
        m_t = jnp.maximum(inter, jnp.max(dmat, axis=0, keepdims=True))
        w = jnp.exp(dmat - m_t)
        a_inter = jnp.exp(inter - m_t)
        qt_h = q_t[h * dk:(h + 1) * dk, :]
        k_h = k[:, h * dk:(h + 1) * dk]
        s = (_dot(k_h, qt_h) * w).astype(BF16)
        vt_aug = jnp.concatenate([v_t[h * dv:(h + 1) * dv, :], ones_row], axis=0)
        st = st_ref[h]
        r = a_inter * _dot(st.astype(BF16), qt_h) + _dot(vt_aug, s)
        den = r[dv:dv + 1, :]
        out_ref[0, 0, h * dv:(h + 1) * dv, :] = r[:dv] / jnp.maximum(jnp.abs(den), jnp.exp(-m_t))
        b_last = b_tot[h:h + 1, :]
        g = b_last - b_t + li
        m_new = jnp.maximum(b_last + m_prev, jnp.max(g, axis=-1, keepdims=True))
        decay = jnp.exp(b_last + m_prev - m_new)
        wg = jnp.exp(g - m_new)
        st_ref[h] = decay * st + _dot((vt_aug.astype(F32) * wg).astype(BF16), k_h)
        m_ref[h] = jnp.broadcast_to(m_new, (SUBLANES, LANES))


def _mlstm_scan(q_t, k, v_t, gcol, grow, lay):
    b, n = lay.batch, lay.n
    L = MLSTM_CHUNK
    n_chunks, ctx_chunks = n // L, lay.n_ctx // L

    def cidx(d, j):
        return _scan_chunk_index(d == 1, j, n_chunks, ctx_chunks)

    return pl.pallas_call(
        _mlstm_scan_kernel,
        grid=(b, 2, n_chunks),
        in_specs=[
            pl.BlockSpec((1, MLSTM_QK, L), lambda bb, d, j: (bb, 0, cidx(d, j))),
            pl.BlockSpec((1, L, MLSTM_QK), lambda bb, d, j: (bb, cidx(d, j), 0)),
            pl.BlockSpec((1, D_MODEL, L), lambda bb, d, j: (bb, 0, cidx(d, j))),
            pl.BlockSpec((1, L, LANES), lambda bb, d, j: (bb, cidx(d, j), 0)),
            pl.BlockSpec((1, 4 * MLSTM_HEADS, L), lambda bb, d, j: (bb, 0, cidx(d, j))),
        ],
        out_specs=pl.BlockSpec((1, 1, D_MODEL, L), lambda bb, d, j: (d, bb, 0, cidx(d, j))),
        out_shape=jax.ShapeDtypeStruct((2, b, D_MODEL, n), F32),
        scratch_shapes=[
            pltpu.VMEM((MLSTM_HEADS, MLSTM_DV + LANES, MLSTM_DQK), F32),
            pltpu.VMEM((MLSTM_HEADS, SUBLANES, LANES), F32),
        ],
        compiler_params=_cparams(("parallel", "parallel", "arbitrary")),
        name="mlstm_scan",
    )(q_t, k, v_t, gcol, grow)


def _gqa_in_kernel(h_ref, g_ref, mod_ref, w_ref, qg_ref, kg_ref, cos_ref, sin_ref, q_ref, k_ref, v_ref):
    hd = ATTN_HEAD_DIM
    mod = mod_ref[0]
    u = _norm_mod(h_ref[0], g_ref[...], mod[0:1], mod[1:2]).astype(BF16)
    p = _dot(u, w_ref[...])
    cos = cos_ref[...]
    sin = sin_ref[...]
    lane = lax.broadcasted_iota(I32, cos.shape, 1)
    first_half = (lane & (hd // 4)) == 0

    def rope(xh):
        swapped = jnp.where(first_half, pltpu.roll(xh, hd - hd // 4, 1), pltpu.roll(xh, hd // 4, 1))
        return xh * cos + swapped * sin

    nq, nk = ATTN_HEADS * hd, ATTN_KV_HEADS * hd
    qn = _head_rms(p[:, :nq], hd)
    kn = _head_rms(p[:, nq:nq + nk], hd)
    qg = qg_ref[...]
    kg = kg_ref[...]
    for i in range(ATTN_HEADS):
        q_ref[0, :, i * hd:(i + 1) * hd] = (rope(qn[:, i * hd:(i + 1) * hd] * qg) * ATTN_Q_SCALE).astype(BF16)
    for i in range(ATTN_KV_HEADS):
        k_ref[0, :, i * hd:(i + 1) * hd] = rope(kn[:, i * hd:(i + 1) * hd] * kg).astype(BF16)
    v_ref[0] = p[:, nq + nk:].astype(BF16)


def _rope_tables(lay):
    hd = ATTN_HEAD_DIM
    quarter = hd // 4
    inv = ROPE_THETA ** (-jnp.arange(quarter, dtype=F32) / quarter)
    pos = jnp.arange(lay.n_lat)
    rows = (pos // GRID_W).astype(F32)
    cols = (pos % GRID_W).astype(F32)
    ang = jnp.concatenate([jnp.tile(rows[:, None] * inv, (1, 2)), jnp.tile(cols[:, None] * inv, (1, 2))], axis=1)
    sign = jnp.tile(jnp.concatenate([-jnp.ones(quarter, F32), jnp.ones(quarter, F32)]), 2)
    cos = jnp.concatenate([jnp.ones((lay.n_ctx, hd), F32), jnp.cos(ang)], axis=0)
    sin = jnp.concatenate([jnp.zeros((lay.n_ctx, hd), F32), jnp.sin(ang) * sign], axis=0)
    return cos, sin


def _gqa_in(h, norm_g, mod_l, w_in, q_g, k_g, lay):
    d, hd = D_MODEL, ATTN_HEAD_DIM
    nq, nk = ATTN_HEADS * hd, ATTN_KV_HEADS * hd
    cos, sin = _rope_tables(lay)
    b, n = lay.batch, lay.n
    tab_spec = pl.BlockSpec((TOKEN_TILE, hd), lambda bb, t: (t, 0))
    return pl.pallas_call(
        _gqa_in_kernel,
        grid=(b, lay.tiles),
        in_specs=[
            _tok_spec(d, lay), _full_spec((1, d)), _mod_spec(lay), _full_spec((d, nq + 2 * nk)),
            _full_spec((1, hd)), _full_spec((1, hd)), tab_spec, tab_spec,
        ],
        out_specs=[_tok_spec(nq, lay), _tok_spec(nk, lay), _tok_spec(nk, lay)],
        out_shape=[
            jax.ShapeDtypeStruct((b, n, nq), BF16),
            jax.ShapeDtypeStruct((b, n, nk), BF16),
            jax.ShapeDtypeStruct((b, n, nk), BF16),
        ],
        compiler_params=_cparams(("parallel", "parallel")),
        name="gqa_in",
    )(h, norm_g.reshape(1, d), mod_l, w_in.astype(BF16), q_g.reshape(1, hd), k_g.reshape(1, hd), cos, sin)


def _attn_kernel(q_ref, k_ref, v_ref, o_ref, *, ctx_tiles, n_ctx, n_all):
    hd = ATTN_HEAD_DIM
    t = pl.program_id(2)
    q = q_ref[0]
    rows = q.shape[0]
    q2 = jnp.concatenate([q[:, g * hd:(g + 1) * hd] for g in range(ATTN_GROUP)], axis=0)

    def attend(n_keys):
        bounds = [0, n_ctx] + list(range(n_ctx + ATTN_KV_CHUNK, n_keys + 1, ATTN_KV_CHUNK))
        assert bounds[-1] == n_keys
        m = l = acc = None
        for lo, hi in zip(bounds[:-1], bounds[1:]):
            s = _dot_nt(q2, k_ref[0, lo:hi, :])
            mc = jnp.max(s, axis=-1, keepdims=True)
            if m is None:
                m_new = mc
                p = jnp.exp2(s - m_new)
                l = jnp.sum(p, axis=-1, keepdims=True)
                acc = _dot(p.astype(BF16), v_ref[0, lo:hi, :])
            else:
                m_new = jnp.maximum(m, mc)
                alpha = jnp.exp2(m - m_new)
                p = jnp.exp2(s - m_new)
                l = alpha * l + jnp.sum(p, axis=-1, keepdims=True)
                acc = alpha * acc + _dot(p.astype(BF16), v_ref[0, lo:hi, :])
            m = m_new
        o = acc / l
        for g in range(ATTN_GROUP):
            o_ref[0, :, g * hd:(g + 1) * hd] = o[g * rows:(g + 1) * rows].astype(BF16)

    @pl.when(t < ctx_tiles)
    def _():
        attend(n_ctx)

    @pl.when(t >= ctx_tiles)
    def _():
        attend(n_all)


def _attention(q, k, v, lay):
    hd = ATTN_HEAD_DIM
    b, n = lay.batch, lay.n
    gw = ATTN_GROUP * hd
    kern = functools.partial(_attn_kernel, ctx_tiles=lay.ctx_tiles, n_ctx=lay.n_ctx, n_all=n)
    return pl.pallas_call(
        kern,
        grid=(b, ATTN_KV_HEADS, lay.tiles),
        in_specs=[
            pl.BlockSpec((1, TOKEN_TILE, gw), lambda bb, kh, t: (bb, t, kh)),
            pl.BlockSpec((1, n, hd), lambda bb, kh, t: (bb, 0, kh)),
            pl.BlockSpec((1, n, hd), lambda bb, kh, t: (bb, 0, kh)),
        ],
        out_specs=pl.BlockSpec((1, TOKEN_TILE, gw), lambda bb, kh, t: (bb, t, kh)),
        out_shape=jax.ShapeDtypeStruct((b, n, ATTN_HEADS * hd), BF16),
        compiler_params=_cparams(("parallel", "parallel", "parallel")),
        name="attention",
    )(q, k, v)


def _gla_in_kernel(h_ref, g_ref, mod_ref, w_ref, wa_ref, aw_ref, ab_ref,
                   q_ref, k_ref, v_ref, gate_ref, laf_ref, lab_ref):
    d = D_MODEL
    mod = mod_ref[0]
    u = _norm_mod(h_ref[0], g_ref[...], mod[0:1], mod[1:2]).astype(BF16)
    p = _dot(u, w_ref[...])
    q_ref[0] = p[:, :GLA_QK] * GLA_DK ** -0.5
    k_ref[0] = p[:, GLA_QK:2 * GLA_QK]
    v_ref[0] = p[:, 2 * GLA_QK:2 * GLA_QK + d].astype(BF16)
    gate_ref[0] = p[:, 2 * GLA_QK + d:]
    a_low = _dot(u, wa_ref[...]).astype(BF16)
    pre = _dot(a_low, aw_ref[...]) + ab_ref[...]
    la = _log_sigmoid(pre) * (1.0 / GLA_TAU)
    laf_ref[0] = la[:, :GLA_QK]
    lab_ref[0] = la[:, GLA_QK:]


def _gla_in(h, norm_g, mod_l, w_in, alpha_w2, alpha_b, lay):
    d, r = D_MODEL, GLA_GATE_RANK
    n_main = 2 * GLA_QK + 2 * d
    w_main = w_in[:, :n_main].astype(BF16)
    wa = jnp.zeros((d, LANES), F32).at[:, :2 * r].set(w_in[:, n_main:]).astype(BF16)
    aw = jnp.zeros((LANES, 2 * GLA_QK), F32)
    aw = aw.at[:r, :GLA_QK].set(alpha_w2[0]).at[r:2 * r, GLA_QK:].set(alpha_w2[1]).astype(BF16)
    ab = alpha_b.reshape(1, 2 * GLA_QK)
    b, n = lay.batch, lay.n
    return pl.pallas_call(
        _gla_in_kernel,
        grid=(b, lay.tiles),
        in_specs=[
            _tok_spec(d, lay), _full_spec((1, d)), _mod_spec(lay), _full_spec((d, n_main)),
            _full_spec((d, LANES)), _full_spec((LANES, 2 * GLA_QK)), _full_spec((1, 2 * GLA_QK)),
        ],
        out_specs=[_tok_spec(GLA_QK, lay), _tok_spec(GLA_QK, lay), _tok_spec(d, lay), _tok_spec(d, lay),
                   _tok_spec(GLA_QK, lay), _tok_spec(GLA_QK, lay)],
        out_shape=[
            jax.ShapeDtypeStruct((b, n, GLA_QK), F32),
            jax.ShapeDtypeStruct((b, n, GLA_QK), F32),
            jax.ShapeDtypeStruct((b, n, d), BF16),
            jax.ShapeDtypeStruct((b, n, d), F32),
            jax.ShapeDtypeStruct((b, n, GLA_QK), F32),
            jax.ShapeDtypeStruct((b, n, GLA_QK), F32),
        ],
        compiler_params=_cparams(("parallel", "parallel")),
        name="gla_in",
    )(h, norm_g.reshape(1, d), mod_l, w_main, wa, aw, ab)


def _gla_scan_kernel(q_ref, k_ref, v_ref, la_ref, out_ref, st_ref, *, reverse):
    T, C = GLA_TILE, GLA_CHUNK
    dk, dv, nh = GLA_DK, GLA_DV, GLA_HEADS
    j = pl.program_id(1)

    @pl.when(j == 0)
    def _():
        st_ref[...] = jnp.zeros_like(st_ref)

    n_chunks = T // C
    row = lax.broadcasted_iota(I32, (T, T), 0)
    col = lax.broadcasted_iota(I32, (T, T), 1)
    before = (col >= row) if reverse else (col <= row)
    b = jnp.dot(before.astype(F32), la_ref[0], precision=HIGHEST, preferred_element_type=F32)
    b_end = b[0:1] if reverse else b[T - 1:T]
    q = q_ref[0]
    k = k_ref[0]
    q_state = (q * jnp.exp(b)).astype(BF16)
    k_state = (k * jnp.exp(b_end - b)).astype(BF16)
    decay = jnp.exp(b_end)

    def rows_of(c):
        return slice(c * C, (c + 1) * C)

    b_mid = jnp.concatenate(
        [jnp.broadcast_to(b[c * C + C // 2:c * C + C // 2 + 1], (C, b.shape[1])) for c in range(n_chunks)], axis=0)
    q_mid = (q * jnp.exp(b - b_mid)).astype(BF16)
    k_mid = (k * jnp.exp(b_mid - b)).astype(BF16)
    same_chunk = (row // C) == (col // C)
    scan_order = list(reversed(range(n_chunks))) if reverse else list(range(n_chunks))
    later = scan_order[1:]
    q_off, k_off, off_mask = {}, {}, {}
    for c in later:
        edge = (c + 1) * C if reverse else c * C - 1
        b_edge = b[edge:edge + 1]
        q_off[c] = (q[rows_of(c)] * jnp.exp(b[rows_of(c)] - b_edge)).astype(BF16)
        k_off[c] = (k * jnp.exp(jnp.minimum(b_edge - b, 0.0))).astype(BF16)
        ccol = lax.broadcasted_iota(I32, (C, T), 1)
        off_mask[c] = (ccol >= (c + 1) * C) if reverse else (ccol < c * C)

    for h in range(nh):
        ks = slice(h * dk, (h + 1) * dk)
        vs = slice(h * dv, (h + 1) * dv)
        v_h = v_ref[0, :, vs]
        a_same = jnp.where(same_chunk & before, _dot_nt(q_mid[:, ks], k_mid[:, ks]), 0.0)
        blocks = []
        for c in range(n_chunks):
            a_c = a_same[rows_of(c)]
            if c in q_off:
                a_c = a_c + jnp.where(off_mask[c], _dot_nt(q_off[c][:, ks], k_off[c][:, ks]), 0.0)
            blocks.append(a_c)
        a = jnp.concatenate(blocks, axis=0).astype(BF16)
        st = st_ref[h]
        out_ref[0, :, vs] = _dot_nt(q_state[:, ks], st.astype(BF16)) + _dot(a, v_h)
        st_ref[h] = st * decay[:, ks] + _dot_tn(v_h, k_state[:, ks])


def _gla_scan(q, k, v, la, lay, reverse):
    b, n = lay.batch, lay.n
    n_tiles, ctx_tiles = n // GLA_TILE, lay.n_ctx // GLA_TILE

    def tidx(j):
        return _scan_chunk_index(reverse, j, n_tiles, ctx_tiles)

    kern = functools.partial(_gla_scan_kernel, reverse=reverse)
    return pl.pallas_call(
        kern,
        grid=(b, n_tiles),
        in_specs=[
            pl.BlockSpec((1, GLA_TILE, GLA_QK), lambda bb, j: (bb, tidx(j), 0)),
            pl.BlockSpec((1, GLA_TILE, GLA_QK), lambda bb, j: (bb, tidx(j), 0)),
            pl.BlockSpec((1, GLA_TILE, D_MODEL), lambda bb, j: (bb, tidx(j), 0)),
            pl.BlockSpec((1, GLA_TILE, GLA_QK), lambda bb, j: (bb, tidx(j), 0)),
        ],
        out_specs=pl.BlockSpec((1, GLA_TILE, D_MODEL), lambda bb, j: (bb, tidx(j), 0)),
        out_shape=jax.ShapeDtypeStruct((b, n, D_MODEL), F32),
        scratch_shapes=[pltpu.VMEM((GLA_HEADS, GLA_DV, GLA_DK), F32)],
        compiler_params=_cparams(("parallel", "arbitrary")),
        name="gla_scan_bwd" if reverse else "gla_scan_fwd",
    )(q, k, v, la)


def _readout_kernel(*refs, mode, first_routed_tile):
    if mode == "attn":
        a_ref, wout_ref = refs[:2]
        rest = refs[2:]
        y_in = a_ref[0]
    else:
        a_ref, b_ref, gsrc_ref, ng_ref, wout_ref = refs[:5]
        rest = refs[5:]
        hs = a_ref[0] + b_ref[0]
        if mode == "mlstm":
            normed = []
            for i in range(MLSTM_HEADS):
                xs = hs[i * MLSTM_DV:(i + 1) * MLSTM_DV, :]
                ms = jnp.mean(xs * xs, axis=0, keepdims=True)
                normed.append(xs * lax.rsqrt(ms + NORM_EPS))
            y_in = (jnp.concatenate(normed, axis=0) * ng_ref[...] * _sigmoid(gsrc_ref[0])).T
        else:
            gsrc = gsrc_ref[0]
            y_in = _head_rms(hs, GLA_DV) * ng_ref[...] * (gsrc * _sigmoid(gsrc))
        y_in = y_in.astype(BF16)
    (h_ref, mod_ref, n2g_ref, wrh_ref, wrl_ref, br_ref,
     hnew_ref, v_ref, e_ref, gate_ref, rank_ref, cnt_ref, carry_ref) = rest

    first = (pl.program_id(0) == 0) & (pl.program_id(1) == 0)

    @pl.when(first)
    def _():
        carry_ref[...] = jnp.zeros_like(carry_ref)

    mod = mod_ref[0]
    hn = h_ref[0] + mod[2:3] * _dot(y_in, wout_ref[...])
    hnew_ref[0] = hn
    v = _norm_mod(hn, n2g_ref[...], mod[3:4], mod[4:5])
    for c in range(ROW_TILE):
        v_ref[0, pl.ds(c, v.shape[0], stride=ROW_TILE), :] = v[:, c * LANES:(c + 1) * LANES]

    v_hi = v.astype(BF16)
    v_lo = (v - v_hi.astype(F32)).astype(BF16)
    w_hi = wrh_ref[...]
    logits = _dot(v_hi, w_hi) + _dot(v_lo, w_hi) + _dot(v_hi, wrl_ref[...]) + br_ref[...]

    tm = logits.shape[0]
    lane = lax.broadcasted_iota(I32, logits.shape, 1)
    lane_f = lane.astype(F32)
    work = logits
    top_e = jnp.zeros(logits.shape, F32)
    top_p = jnp.zeros(logits.shape, F32)
    onehot = jnp.zeros(logits.shape, F32)
    m0 = None
    hits = []
    for jx in range(TOP_K):
        mx = jnp.max(work, axis=-1, keepdims=True)
        idx = jnp.min(jnp.where(work == mx, lane_f, float(LANES)), axis=-1, keepdims=True)
        if jx == 0:
            m0 = mx
        hit = lane_f == idx
        hits.append(hit)
        top_e = jnp.where(lane == jx, idx, top_e)
        top_p = jnp.where(lane == jx, jnp.exp(mx - m0), top_p)
        onehot = jnp.where(hit, 1.0, onehot)
        work = jnp.where(hit, -jnp.inf, work)
    e_ref[0] = top_e.astype(I32)
    gate_ref[0] = top_p / jnp.sum(top_p, axis=-1, keepdims=True)
    if first_routed_tile:
        onehot = onehot * (pl.program_id(1) >= first_routed_tile).astype(F32)

    r = lax.broadcasted_iota(I32, (tm, tm), 0)
    c = lax.broadcasted_iota(I32, (tm, tm), 1)
    tril = (c <= r).astype(BF16)
    cum = _dot(tril, onehot.astype(BF16))
    carry = carry_ref[0:1, :]
    rank_all = carry + cum - onehot
    rank_sel = jnp.zeros(logits.shape, F32)
    for jx in range(TOP_K):
        picked = jnp.sum(jnp.where(hits[jx], rank_all, 0.0), axis=-1, keepdims=True)
        rank_sel = jnp.where(lane == jx, picked, rank_sel)
    rank_ref[0] = rank_sel.astype(I32)
    total = carry + cum[tm - 1:tm, :]
    carry_ref[...] = jnp.broadcast_to(total, carry_ref.shape)
    cnt_ref[...] = jnp.broadcast_to(total, cnt_ref.shape)


def _readout(mode, mixer_outs, w_out, h, mod_l, norm2_g, router_w, router_b, lay, latent_only):
    d = D_MODEL
    b, n = lay.batch, lay.n
    wr = jnp.zeros((d, LANES), F32).at[:, :N_EXPERTS].set(router_w)
    wr_hi = wr.astype(BF16)
    wr_lo = (wr - wr_hi.astype(F32)).astype(BF16)
    br = jnp.full((1, LANES), NEG_BIG, F32).at[0, :N_EXPERTS].set(router_b)
    if mode == "attn":
        (attn_o,) = mixer_outs
        head_in = [attn_o]
        head_specs = [_tok_spec(d, lay)]
    elif mode == "mlstm":
        h_dir, o_gate, norm_g = mixer_outs
        head_in = [h_dir, h_dir, o_gate, norm_g.reshape(d, 1)]
        head_specs = [pl.BlockSpec((None, 1, d, TOKEN_TILE), lambda bb, t: (0, bb, 0, t)),
                      pl.BlockSpec((None, 1, d, TOKEN_TILE), lambda bb, t: (1, bb, 0, t)),
                      _feat_spec(d), _full_spec((d, 1))]
    else:
        o_f, o_b, g_gate, norm_g = mixer_outs
        head_in = [o_f, o_b, g_gate, norm_g.reshape(1, d)]
        head_specs = [_tok_spec(d, lay), _tok_spec(d, lay), _tok_spec(d, lay), _full_spec((1, d))]
    kern = functools.partial(_readout_kernel, mode=mode, first_routed_tile=lay.ctx_tiles if latent_only else 0)
    return pl.pallas_call(
        kern,
        grid=(b, lay.tiles),
        in_specs=head_specs + [
            _full_spec((d, d)), _tok_spec(d, lay), _mod_spec(lay), _full_spec((1, d)),
            _full_spec((d, LANES)), _full_spec((d, LANES)), _full_spec((1, LANES)),
        ],
        out_specs=[_tok_spec(d, lay), pl.BlockSpec((1, TOKEN_TILE * ROW_TILE, LANES), lambda bb, t: (bb, t, 0)),
                   _tok_spec(LANES, lay), _tok_spec(LANES, lay),
                   _tok_spec(LANES, lay), _full_spec((SUBLANES, LANES))],
        out_shape=[
            jax.ShapeDtypeStruct((b, n, d), F32),
            jax.ShapeDtypeStruct((b, n * ROW_TILE, LANES), F32),
            jax.ShapeDtypeStruct((b, n, LANES), I32),
            jax.ShapeDtypeStruct((b, n, LANES), F32),
            jax.ShapeDtypeStruct((b, n, LANES), I32),
            jax.ShapeDtypeStruct((SUBLANES, LANES), F32),
        ],
        scratch_shapes=[pltpu.VMEM((SUBLANES, LANES), F32)],
        compiler_params=_cparams(("arbitrary", "arbitrary")),
        name="readout_" + mode,
    )(*head_in, w_out.astype(BF16), h, mod_l, norm2_g.reshape(1, d), wr_hi, wr_lo, br)


def _slot(ps_ref, e_ref, r_ref, i):
    return ps_ref[e_ref[0, 0, i]] + r_ref[0, 0, i]


def _row_copies_wait(src_block, dst_rows, sem):
    for _ in range(TOP_K):
        pltpu.make_async_copy(src_block, dst_rows, sem).wait()


def _zero_fill(zs_ref, nu_ref, xb_ref, zbuf_ref, sem):
    zbuf_ref[...] = jnp.zeros_like(zbuf_ref)
    blk_rows = zbuf_ref.shape[0]
    n_blocks = xb_ref.shape[0] // blk_rows

    def zero_block(start):
        return pltpu.make_async_copy(zbuf_ref, xb_ref.at[pl.ds(start, blk_rows), :], sem)

    def trailing(i, then):
        pl.when(i >= nu_ref[0])(lambda: then(zero_block(i * blk_rows)))

    for e in range(N_EXPERTS):
        zero_block(pl.multiple_of(zs_ref[e] * ROW_TILE, ROW_TILE)).start()
    for e in range(N_EXPERTS):
        zero_block(0).wait()
    for i in range(n_blocks - N_EXPERTS, n_blocks):
        trailing(i, lambda cp: cp.start())
    for i in range(n_blocks - N_EXPERTS, n_blocks):
        trailing(i, lambda cp: cp.wait())


def _dispatch_kernel(e_ref, r_ref, ps_ref, zs_ref, nu_ref, v_ref, xb_ref, zbuf_ref, sem, *, first_routed_tile):
    rows = v_ref.shape[1] // ROW_TILE

    @pl.when((pl.program_id(0) == 0) & (pl.program_id(1) == 0))
    def _():
        _zero_fill(zs_ref, nu_ref, xb_ref, zbuf_ref, sem)

    def body(r, carry):
        src = v_ref.at[0, pl.ds(pl.multiple_of(r * ROW_TILE, ROW_TILE), ROW_TILE), :]
        for jx in range(TOP_K):
            slot = _slot(ps_ref, e_ref, r_ref, r * TOP_K + jx)
            dst = xb_ref.at[pl.ds(pl.multiple_of(slot * ROW_TILE, ROW_TILE), ROW_TILE), :]
            pltpu.make_async_copy(src, dst, sem).start(priority=jx % DMA_THREADS)
        return carry

    @pl.when(pl.program_id(1) >= first_routed_tile)
    def _():
        lax.fori_loop(0, rows, body, 0, unroll=DMA_UNROLL)
        _row_copies_wait(v_ref.at[0], xb_ref.at[pl.ds(0, rows * ROW_TILE), :], sem)


def _assign_spec(lay):
    tiles = lay.tiles
    return pl.BlockSpec((1, 1, TOKEN_TILE * TOP_K), lambda bb, t: (bb * tiles + t, 0, 0), memory_space=pltpu.SMEM)


def _dispatch(e_c, r_c, pad_starts, zero_start, n_used, v_rows, cap, lay, latent_only):
    return pl.pallas_call(
        functools.partial(_dispatch_kernel, first_routed_tile=lay.ctx_tiles if latent_only else 0),
        grid=(lay.batch, lay.tiles),
        in_specs=[
            _assign_spec(lay), _assign_spec(lay),
            pl.BlockSpec(memory_space=pltpu.SMEM),
            pl.BlockSpec(memory_space=pltpu.SMEM),
            pl.BlockSpec(memory_space=pltpu.SMEM),
            pl.BlockSpec((1, TOKEN_TILE * ROW_TILE, LANES), lambda bb, t: (bb, t, 0)),
        ],
        out_specs=pl.BlockSpec(memory_space=pl.ANY),
        out_shape=jax.ShapeDtypeStruct((cap * ROW_TILE, LANES), F32),
        scratch_shapes=[pltpu.VMEM((MOE_BLOCK * ROW_TILE, LANES), F32), pltpu.SemaphoreType.DMA(())],
        compiler_params=_cparams(("arbitrary", "arbitrary")),
        name="moe_dispatch",
    )(e_c, r_c, pad_starts, zero_start, n_used, v_rows)


def _expert_kernel(be_ref, nu_ref, x_ref, wgu_ref, bgu_ref, wd_ref, bd_ref, y_ref, wgu_bf_ref, wd_bf_ref):
    i = pl.program_id(0)
    e = be_ref[i]
    used = i < nu_ref[0]
    new_expert = (i == 0) | (e != be_ref[jnp.maximum(i - 1, 0)])

    @pl.when(new_expert)
    def _():
        wgu_bf_ref[...] = wgu_ref[0, 0].astype(BF16)
        wd_bf_ref[...] = wd_ref[0, 0].astype(BF16)

    @pl.when(jnp.logical_not(used))
    def _():
        y_ref[...] = jnp.zeros_like(y_ref)

    @pl.when(used)
    def _():
        x = jnp.concatenate([x_ref[pl.ds(c, MOE_BLOCK, stride=ROW_TILE), :] for c in range(ROW_TILE)], axis=1)
        gu = _dot(x.astype(BF16), wgu_bf_ref[...]) + bgu_ref[0, 0]
        g = jnp.minimum(gu[:, :D_EXPERT], SWIGLU_LIMIT)
        lin = jnp.clip(gu[:, D_EXPERT:], -SWIGLU_LIMIT, SWIGLU_LIMIT)
        hdn = g * _sigmoid(SWIGLU_ALPHA * g) * (lin + 1.0)
        y = _dot(hdn.astype(BF16), wd_bf_ref[...]) + bd_ref[0, 0]
        for c in range(ROW_TILE):
            y_ref[pl.ds(c, MOE_BLOCK, stride=ROW_TILE), :] = y[:, c * LANES:(c + 1) * LANES]


def _experts(block_e, n_used, xb, w_gu, b_gu, w_d, b_d, layer):
    d, de = D_MODEL, D_EXPERT
    depth = w_gu.shape[0]
    blk_rows = MOE_BLOCK * ROW_TILE
    n_blocks = xb.shape[0] // blk_rows

    def row_map(i, be, nu):
        return (jnp.minimum(i, nu[0] - 1), 0)

    def exp_map(i, be, nu):
        return (layer, be[i], 0, 0)

    grid_spec = pltpu.PrefetchScalarGridSpec(
        num_scalar_prefetch=2,
        grid=(n_blocks,),
        in_specs=[
            pl.BlockSpec((blk_rows, LANES), row_map),
            pl.BlockSpec((1, 1, d, 2 * de), exp_map),
            pl.BlockSpec((1, 1, 1, 2 * de), exp_map),
            pl.BlockSpec((1, 1, de, d), exp_map),
            pl.BlockSpec((1, 1, 1, d), exp_map),
        ],
        out_specs=pl.BlockSpec((blk_rows, LANES), lambda i, be, nu: (i, 0)),
        scratch_shapes=[pltpu.VMEM((d, 2 * de), BF16), pltpu.VMEM((de, d), BF16)],
    )
    return pl.pallas_call(
        _expert_kernel,
        grid_spec=grid_spec,
        out_shape=jax.ShapeDtypeStruct(xb.shape, F32),
        compiler_params=_cparams(("arbitrary",), vmem=EXPERT_VMEM_LIMIT),
        name="moe_experts",
    )(block_e, n_used, xb, w_gu, b_gu.reshape(depth, N_EXPERTS, 1, 2 * de), w_d,
      b_d.reshape(depth, N_EXPERTS, 1, d))


def _combine_kernel(e_ref, r_ref, en_ref, rn_ref, gate_ref, ps_ref, yb_ref, h_ref, mod_ref, out_ref,
                    ybuf_ref, acc_ref, sems, *, tiles_per_sample, n_tiles):
    rows = h_ref.shape[1]
    tile = pl.program_id(0) * tiles_per_sample + pl.program_id(1)
    buf = tile % 2

    def issue_row(e_src, r_src, into, r):
        base = pl.multiple_of(r * ROW_TILE, ROW_TILE)
        for jx in range(TOP_K):
            slot = _slot(ps_ref, e_src, r_src, r * TOP_K + jx)
            src = yb_ref.at[pl.ds(pl.multiple_of(slot * ROW_TILE, ROW_TILE), ROW_TILE), :]
            pltpu.make_async_copy(src, ybuf_ref.at[into, jx, pl.ds(base, ROW_TILE), :], sems.at[into]).start(
                priority=jx % DMA_THREADS)

    def accumulate_row(r):
        base = pl.multiple_of(r * ROW_TILE, ROW_TILE)
        a = gate_ref[0, 0, r * TOP_K] * ybuf_ref[buf, 0, pl.ds(base, ROW_TILE), :]
        for jx in range(1, TOP_K):
            a = a + gate_ref[0, 0, r * TOP_K + jx] * ybuf_ref[buf, jx, pl.ds(base, ROW_TILE), :]
        acc_ref[pl.ds(base, ROW_TILE), :] = a

    def rows_loop(body):
        def step(r, carry):
            body(r)
            return carry
        lax.fori_loop(0, rows, step, 0, unroll=DMA_UNROLL)

    @pl.when(tile == 0)
    def _():
        rows_loop(lambda r: issue_row(e_ref, r_ref, 0, r))

    _row_copies_wait(yb_ref.at[pl.ds(0, rows * ROW_TILE), :], ybuf_ref.at[buf, 0], sems.at[buf])

    @pl.when(tile + 1 < n_tiles)
    def _():
        def both(r):
            issue_row(en_ref, rn_ref, 1 - buf, r)
            accumulate_row(r)
        rows_loop(both)

    @pl.when(tile + 1 >= n_tiles)
    def _():
        rows_loop(accumulate_row)

    for c in range(ROW_TILE):
        cs = slice(c * LANES, (c + 1) * LANES)
        out_ref[0, :, cs] = h_ref[0, :, cs] + mod_ref[0, 5:6, cs] * acc_ref[pl.ds(c, rows, stride=ROW_TILE), :]


def _combine(e_c, r_c, gate_c, pad_starts, yb, h, mod_l, lay, latent_only):
    d = D_MODEL
    tiles = lay.tiles
    n_tiles = lay.batch * tiles
    if latent_only:
        out_spec = pl.BlockSpec((1, TOKEN_TILE, d), lambda bb, t: (bb, jnp.maximum(t - lay.ctx_tiles, 0), 0))
        out_rows = lay.n_lat
    else:
        out_spec = _tok_spec(d, lay)
        out_rows = lay.n
    next_spec = pl.BlockSpec((1, 1, TOKEN_TILE * TOP_K),
                             lambda bb, t: (jnp.minimum(bb * tiles + t + 1, n_tiles - 1), 0, 0),
                             memory_space=pltpu.SMEM)
    return pl.pallas_call(
        functools.partial(_combine_kernel, tiles_per_sample=tiles, n_tiles=n_tiles),
        grid=(lay.batch, tiles),
        in_specs=[_assign_spec(lay), _assign_spec(lay), next_spec, next_spec, _assign_spec(lay),
                  pl.BlockSpec(memory_space=pltpu.SMEM),
                  pl.BlockSpec(memory_space=pl.ANY), _tok_spec(d, lay), _mod_spec(lay)],
        out_specs=out_spec,
        out_shape=jax.ShapeDtypeStruct((lay.batch, out_rows, d), F32),
        scratch_shapes=[pltpu.VMEM((2, TOP_K, TOKEN_TILE * ROW_TILE, LANES), F32),
                        pltpu.VMEM((TOKEN_TILE * ROW_TILE, LANES), F32), pltpu.SemaphoreType.DMA((2,))],
        compiler_params=_cparams(("arbitrary", "arbitrary")),
        name="moe_combine",
    )(e_c, r_c, e_c, r_c, gate_c, pad_starts, yb, h, mod_l)


def _moe(h, v_rows, top_e, gate, rank, counts, mod_l, w_gu, b_gu, w_d, b_d, layer, lay, latent_only):
    n_tokens = lay.batch * lay.n
    n_assign = n_tokens * TOP_K
    cap = (-(-n_assign // MOE_BLOCK) + N_EXPERTS) * MOE_BLOCK
    n_blocks = cap // MOE_BLOCK
    cnt = counts[0, :N_EXPERTS].astype(I32)
    padded = (cnt + MOE_BLOCK - 1) // MOE_BLOCK * MOE_BLOCK
    pad_ends = jnp.cumsum(padded)
    pad_starts = pad_ends - padded
    n_used = (pad_ends[-1] // MOE_BLOCK).astype(I32).reshape(1)
    blk = jnp.minimum(jnp.arange(n_blocks, dtype=I32), n_used[0] - 1) * MOE_BLOCK
    block_e = jnp.minimum(jnp.sum((pad_ends[None, :] <= blk[:, None]).astype(I32), axis=1), N_EXPERTS - 1)
    pad_starts = pad_starts.astype(I32)
    zero_start = jnp.minimum(pad_starts + cnt, cap - MOE_BLOCK).astype(I32)

    def per_assignment(a):
        return a[:, :, :TOP_K].reshape(lay.batch * lay.tiles, 1, TOKEN_TILE * TOP_K)

    e_c, r_c, gate_c = per_assignment(top_e), per_assignment(rank), per_assignment(gate)
    xb = _dispatch(e_c, r_c, pad_starts, zero_start, n_used, v_rows, cap, lay, latent_only)
    yb = _experts(block_e, n_used, xb, w_gu, b_gu, w_d, b_d, layer)
    return _combine(e_c, r_c, gate_c, pad_starts, yb, h, mod_l, lay, latent_only)


def kernel(x, c, ctx, c_ctx, norm1_g, norm2_g, mod_w, mod_b,
           mlstm_w_in, mlstm_conv_w, mlstm_gate_b, mlstm_out_norm_g, mlstm_w_out,
           attn_w_in, attn_q_norm_g, attn_k_norm_g, attn_w_out,
           gla_w_in, gla_alpha_w2, gla_alpha_b, gla_out_norm_g, gla_w_out,
           router_w, router_b, moe_w_gu, moe_b_gu, moe_w_down, moe_b_down):
    batch, n_lat, d = x.shape
    n_ctx = ctx.shape[1]
    depth = norm1_g.shape[0]
    assert d == D_MODEL
    lay = _Layout(batch, n_ctx, n_lat)

    cond_rows = -(-(batch + 1) // SUBLANES) * SUBLANES
    cond = jnp.zeros((cond_rows, d), F32).at[:batch].set(c).at[batch].set(c_ctx)
    mod_all = _modulation(cond, mod_w, mod_b)

    h = jnp.concatenate([ctx, x], axis=1)
    for layer in range(depth):
        kind, j = layer % 3, layer // 3
        last = layer == depth - 1
        mod_l = mod_all[layer]
        if kind == 0:
            qk_pre, v_t, o_gate, gcol, grow = _mlstm_in(h, norm1_g[layer], mod_l, mlstm_w_in[j],
                                                        mlstm_gate_b[j], lay)
            q_t, k_m = _mlstm_conv(qk_pre, mlstm_conv_w[j], lay)
            h_dir = _mlstm_scan(q_t, k_m, v_t, gcol, grow, lay)
            mixer = ("mlstm", (h_dir, o_gate, mlstm_out_norm_g[j]), mlstm_w_out[j])
        elif kind == 1:
            q_a, k_a, v_a = _gqa_in(h, norm1_g[layer], mod_l, attn_w_in[j], attn_q_norm_g[j],
                                    attn_k_norm_g[j], lay)
            mixer = ("attn", (_attention(q_a, k_a, v_a, lay),), attn_w_out[j])
        else:
            q_g, k_g, v_g, g_gate, la_f, la_b = _gla_in(h, norm1_g[layer], mod_l, gla_w_in[j],
                                                        gla_alpha_w2[j], gla_alpha_b[j], lay)
            o_f = _gla_scan(q_g, k_g, v_g, la_f, lay, reverse=False)
            o_b = _gla_scan(q_g, k_g, v_g, la_b, lay, reverse=True)
            mixer = ("gla", (o_f, o_b, g_gate, gla_out_norm_g[j]), gla_w_out[j])
        mode, mixer_outs, w_out = mixer
        h, v_moe, top_e, gate, rank, counts = _readout(mode, mixer_outs, w_out, h, mod_l, norm2_g[layer],
                                                       router_w[layer], router_b[layer], lay, last)
        h = _moe(h, v_moe, top_e, gate, rank, counts, mod_l,
                 moe_w_gu, moe_b_gu, moe_w_down, moe_b_down, layer, lay, latent_only=last)
    return h
```

```python
import functools

import jax
import jax.numpy as jnp
from jax import lax
from jax.experimental import pallas as pl
from jax.experimental.pallas import tpu as pltpu

F32 = jnp.float32
BF16 = jnp.bfloat16
I32 = jnp.int32
HIGHEST = lax.Precision.HIGHEST

D_MODEL = 1024
GRID_W = 64
NORM_EPS = 1e-6

MLSTM_HEADS = 8
MLSTM_DQK = 64
MLSTM_DV = 128
MLSTM_QK = MLSTM_HEADS * MLSTM_DQK

ATTN_HEADS = 8
ATTN_KV_HEADS = 4
ATTN_GROUP = ATTN_HEADS // ATTN_KV_HEADS
ATTN_HEAD_DIM = 128
ROPE_THETA = 10000.0

GLA_HEADS = 4
GLA_DK = 128
GLA_DV = 256
GLA_GATE_RANK = 16
GLA_TAU = 16.0
GLA_QK = GLA_HEADS * GLA_DK

N_EXPERTS = 32
TOP_K = 4
D_EXPERT = D_MODEL
SWIGLU_LIMIT = 7.0
SWIGLU_ALPHA = 1.702

LANES = 128
SUBLANES = 8
TOKEN_TILE = 256
MLSTM_CHUNK = 256
GLA_TILE = 128
GLA_CHUNK = 32
ATTN_KV_CHUNK = 256
ATTN_Q_SCALE = ATTN_HEAD_DIM ** -0.5 * 1.4426950408889634
MOE_BLOCK = 512
ROW_TILE = D_MODEL // LANES
assert ROW_TILE == SUBLANES
DMA_UNROLL = 8
DMA_THREADS = 2
VMEM_LIMIT = 48 * 1024 * 1024
EXPERT_VMEM_LIMIT = 56 * 1024 * 1024
NEG_BIG = -1e30


def _cparams(semantics, vmem=VMEM_LIMIT):
    return pltpu.CompilerParams(dimension_semantics=semantics, vmem_limit_bytes=vmem)


def _log_sigmoid(x):
    return jnp.minimum(x, 0.0) - jnp.log(1.0 + jnp.exp(-jnp.abs(x)))


def _sigmoid(x):
    return 1.0 / (1.0 + jnp.exp(-x))


def _norm_mod(x, g, shift, scale):
    ms = jnp.mean(x * x, axis=-1, keepdims=True)
    y = x * lax.rsqrt(ms + NORM_EPS) * g
    return y * (1.0 + scale) + shift


def _head_rms(x, head_dim):
    outs = []
    for i in range(x.shape[1] // head_dim):
        xs = x[:, i * head_dim:(i + 1) * head_dim]
        ms = jnp.mean(xs * xs, axis=-1, keepdims=True)
        outs.append(xs * lax.rsqrt(ms + NORM_EPS))
    return jnp.concatenate(outs, axis=1)


def _dot(a, b):
    return jnp.dot(a, b, preferred_element_type=F32)


def _dot_nt(a, b):
    return lax.dot_general(a, b, (((1,), (1,)), ((), ())), preferred_element_type=F32)


def _dot_tn(a, b):
    return lax.dot_general(a, b, (((0,), (0,)), ((), ())), preferred_element_type=F32)


def _mod_kernel(c_ref, w_ref, b_ref, o_ref):
    x = c_ref[...]
    x = x * _sigmoid(x)
    o_ref[0, 0] = _dot(x.astype(BF16), w_ref[0].astype(BF16)) + b_ref[0, 0]


def _modulation(cond_rows, mod_w, mod_b):
    depth, d, _ = mod_w.shape
    rows = cond_rows.shape[0]
    out = pl.pallas_call(
        _mod_kernel,
        grid=(depth, 6),
        in_specs=[
            pl.BlockSpec((rows, d), lambda l, j: (0, 0)),
            pl.BlockSpec((1, d, d), lambda l, j: (l, 0, j)),
            pl.BlockSpec((1, 1, 1, d), lambda l, j: (l, j, 0, 0)),
        ],
        out_specs=pl.BlockSpec((1, 1, rows, d), lambda l, j: (l, j, 0, 0)),
        out_shape=jax.ShapeDtypeStruct((depth, 6, rows, d), F32),
        compiler_params=_cparams(("parallel", "parallel")),
        name="modulation",
    )(cond_rows, mod_w, mod_b.reshape(depth, 6, 1, d))
    return out.transpose(0, 2, 1, 3)


class _Layout:
    def __init__(self, batch, n_ctx, n_lat):
        self.batch = batch
        self.n_ctx = n_ctx
        self.n_lat = n_lat
        self.n = n_ctx + n_lat
        assert n_ctx % TOKEN_TILE == 0 and n_lat % TOKEN_TILE == 0
        assert n_ctx % MLSTM_CHUNK == 0 and n_lat % MLSTM_CHUNK == 0
        assert n_ctx % GLA_TILE == 0 and n_lat % GLA_TILE == 0
        self.tiles = self.n // TOKEN_TILE
        self.ctx_tiles = n_ctx // TOKEN_TILE
        self.ctx_row = batch

    def mod_index(self, b, t):
        return jnp.where(t < self.ctx_tiles, self.ctx_row, b)


def _tok_spec(width, lay):
    return pl.BlockSpec((1, TOKEN_TILE, width), lambda b, t: (b, t, 0))


def _feat_spec(features):
    return pl.BlockSpec((1, features, TOKEN_TILE), lambda b, t: (b, 0, t))


def _full_spec(shape):
    nd = len(shape)
    return pl.BlockSpec(shape, lambda b, t: (0,) * nd)


def _mod_spec(lay):
    return pl.BlockSpec((1, 6, D_MODEL), lambda b, t: (lay.mod_index(b, t), 0, 0))


def _mlstm_in_kernel(h_ref, g_ref, mod_ref, w_ref, wvot_ref, wg_ref, wgt_ref, gb_ref, gbt_ref,
                     qk_ref, vt_ref, ot_ref, gc_ref, gr_ref):
    mod = mod_ref[0]
    u = _norm_mod(h_ref[0], g_ref[...], mod[0:1], mod[1:2]).astype(BF16)
    qk_ref[0] = _dot(u, w_ref[...])
    vo_t = _dot_nt(wvot_ref[...], u)
    vt_ref[0] = vo_t[:D_MODEL].astype(BF16)
    ot_ref[0] = vo_t[D_MODEL:]
    gc = _dot(u, wg_ref[...]) + gb_ref[...]
    lane = lax.broadcasted_iota(I32, gc.shape, 1)
    gc_ref[0] = jnp.where(((lane >> 3) & 1) == 1, _log_sigmoid(gc), gc)
    gr = _dot_nt(wgt_ref[...], u) + gbt_ref[...]
    sub = lax.broadcasted_iota(I32, gr.shape, 0)
    gr_ref[0] = jnp.where(((sub >> 3) & 1) == 1, _log_sigmoid(gr), gr)


def _mlstm_in(h, norm_g, mod_l, w_in, gate_b, lay):
    d = D_MODEL
    n_qk = 2 * MLSTM_QK
    n_main = n_qk + 2 * d
    n_gate = 4 * MLSTM_HEADS
    w_qk = w_in[:, :n_qk].astype(BF16)
    w_vo_t = w_in[:, n_qk:n_main].T.astype(BF16)
    w_gate = w_in[:, n_main:]
    wg = jnp.zeros((d, LANES), F32).at[:, :n_gate].set(w_gate).astype(BF16)
    wgt = w_gate.T.astype(BF16)
    gb = jnp.zeros((1, LANES), F32).at[0, :n_gate].set(gate_b.reshape(-1))
    gbt = gate_b.reshape(n_gate, 1)
    b, n = lay.batch, lay.n
    return pl.pallas_call(
        _mlstm_in_kernel,
        grid=(b, lay.tiles),
        in_specs=[
            _tok_spec(d, lay), _full_spec((1, d)), _mod_spec(lay),
            _full_spec((d, n_qk)), _full_spec((2 * d, d)), _full_spec((d, LANES)), _full_spec((n_gate, d)),
            _full_spec((1, LANES)), _full_spec((n_gate, 1)),
        ],
        out_specs=[
            _tok_spec(n_qk, lay), _feat_spec(d), _feat_spec(d), _tok_spec(LANES, lay), _feat_spec(n_gate),
        ],
        out_shape=[
            jax.ShapeDtypeStruct((b, n, n_qk), F32),
            jax.ShapeDtypeStruct((b, d, n), BF16),
            jax.ShapeDtypeStruct((b, d, n), F32),
            jax.ShapeDtypeStruct((b, n, LANES), F32),
            jax.ShapeDtypeStruct((b, n_gate, n), F32),
        ],
        compiler_params=_cparams(("parallel", "parallel")),
        name="mlstm_in",
    )(h, norm_g.reshape(1, d), mod_l, w_qk, w_vo_t, wg, wgt, gb, gbt)


def _mlstm_conv_kernel(x_ref, xp_ref, xn_ref, cw_ref, qt_ref, k_ref, *, tiles, ctx_tiles):
    t = pl.program_id(1)
    x = x_ref[0]
    rows = x.shape[0]
    seg_first = (t == 0) | (t == ctx_tiles)
    seg_last = (t == ctx_tiles - 1) | (t == tiles - 1)
    prev_row = jnp.where(seg_first, 0.0, xp_ref[0, SUBLANES - 1:SUBLANES, :])
    next_row = jnp.where(seg_last, 0.0, xn_ref[0, 0:1, :])
    rid = lax.broadcasted_iota(I32, x.shape, 0)
    x_m1 = jnp.where(rid == 0, prev_row, pltpu.roll(x, 1, 0))
    x_p1 = jnp.where(rid == rows - 1, next_row, pltpu.roll(x, rows - 1, 0))
    cw = cw_ref[...]
    y = cw[0:1] * x_m1 + cw[1:2] * x + cw[2:3] * x_p1
    qk = y * _sigmoid(y)
    qt_ref[0] = (qk[:, :MLSTM_QK] * MLSTM_DQK ** -0.5).T.astype(BF16)
    k_ref[0] = qk[:, MLSTM_QK:].astype(BF16)


def _mlstm_conv(qk_pre, conv_w, lay):
    b, n = lay.batch, lay.n
    n_qk = 2 * MLSTM_QK
    halo = TOKEN_TILE // SUBLANES
    n_halo = n // SUBLANES
    kern = functools.partial(_mlstm_conv_kernel, tiles=lay.tiles, ctx_tiles=lay.ctx_tiles)
    return pl.pallas_call(
        kern,
        grid=(b, lay.tiles),
        in_specs=[
            _tok_spec(n_qk, lay),
            pl.BlockSpec((1, SUBLANES, n_qk), lambda bb, t: (bb, jnp.maximum(t * halo - 1, 0), 0)),
            pl.BlockSpec((1, SUBLANES, n_qk), lambda bb, t: (bb, jnp.minimum((t + 1) * halo, n_halo - 1), 0)),
            _full_spec((3, n_qk)),
        ],
        out_specs=[_feat_spec(MLSTM_QK), _tok_spec(MLSTM_QK, lay)],
        out_shape=[
            jax.ShapeDtypeStruct((b, MLSTM_QK, n), BF16),
            jax.ShapeDtypeStruct((b, n, MLSTM_QK), BF16),
        ],
        compiler_params=_cparams(("parallel", "parallel")),
        name="mlstm_conv",
    )(qk_pre, qk_pre, qk_pre, conv_w)


def _scan_chunk_index(reverse, j, n_chunks, ctx_chunks):
    back = jnp.where(j < ctx_chunks, ctx_chunks - 1 - j, n_chunks - 1 - (j - ctx_chunks))
    return jnp.where(reverse, back, j)


def _mlstm_scan_kernel(qt_ref, k_ref, vt_ref, gc_ref, gr_ref, out_ref, st_ref, m_ref):
    L = MLSTM_CHUNK
    dk, dv, nh = MLSTM_DQK, MLSTM_DV, MLSTM_HEADS
    d = pl.program_id(1)
    j = pl.program_id(2)

    @pl.when(j == 0)
    def _():
        st_ref[...] = jnp.zeros_like(st_ref)
        m_ref[...] = jnp.zeros_like(m_ref)

    q_t = qt_ref[0]
    k = k_ref[0]
    v_t = vt_ref[0]

    gcol = gc_ref[0]
    gcol = jnp.where(d == 0, gcol, pltpu.roll(gcol, LANES - 2 * nh, 1))
    grow = gr_ref[0, pl.ds(pl.multiple_of(d * 2 * nh, 2 * nh), 2 * nh), :]

    row = lax.broadcasted_iota(I32, (L, L), 0)
    col = lax.broadcasted_iota(I32, (L, L), 1)
    sgn = 1 - 2 * d
    mask_t = sgn * (col - row) >= 0
    cum_rows = jnp.dot(grow, mask_t.astype(F32), precision=HIGHEST, preferred_element_type=F32)
    cum_cols = jnp.dot((sgn * (row - col) >= 0).astype(F32), gcol, precision=HIGHEST,
                       preferred_element_type=F32)
    b_minus_i = cum_cols - pltpu.roll(gcol, nh, 1)
    b_tot = jnp.sum(grow[nh:], axis=-1, keepdims=True)

    ones_row = (lax.broadcasted_iota(I32, (LANES, L), 0) == 0).astype(BF16)
    for h in range(nh):
        b_t = cum_rows[nh + h:nh + h + 1, :]
        li = grow[h:h + 1, :]
        m_prev = m_ref[h, 0:1, 0:1]
        dmat = jnp.where(mask_t, b_t - b_minus_i[:, nh + h:nh + h + 1], -jnp.inf)
        inter = b_t + m_prev
        m_t = jnp.maximum(inter, jnp.max(dmat, axis=0, keepdims=True))
        w = jnp.exp(dmat - m_t)
        a_inter = jnp.exp(inter - m_t)
        qt_h = q_t[h * dk:(h + 1) * dk, :]
        k_h = k[:, h * dk:(h + 1) * dk]
        s = (_dot(k_h, qt_h) * w).astype(BF16)
        vt_aug = jnp.concatenate([v_t[h * dv:(h + 1) * dv, :], ones_row], axis=0)
        st = st_ref[h]
        r = a_inter * _dot(st.astype(BF16), qt_h) + _dot(vt_aug, s)
        den = r[dv:dv + 1, :]
        out_ref[0, 0, h * dv:(h + 1) * dv, :] = r[:dv] / jnp.maximum(jnp.abs(den), jnp.exp(-m_t))
        b_last = b_tot[h:h + 1, :]
        g = b_last - b_t + li
        m_new = jnp.maximum(b_last + m_prev, jnp.max(g, axis=-1, keepdims=True))
        decay = jnp.exp(b_last + m_prev - m_new)
        wg = jnp.exp(g - m_new)
        st_ref[h] = decay * st + _dot((vt_aug.astype(F32) * wg).astype(BF16), k_h)
        m_ref[h] = jnp.broadcast_to(m_new, (SUBLANES, LANES))


def _mlstm_scan(q_t, k, v_t, gcol, grow, lay):
    b, n = lay.batch, lay.n
    L = MLSTM_CHUNK
    n_chunks, ctx_chunks = n // L, lay.n_ctx // L

    def cidx(d, j):
        return _scan_chunk_index(d == 1, j, n_chunks, ctx_chunks)

    return pl.pallas_call(
        _mlstm_scan_kernel,
        grid=(b, 2, n_chunks),
        in_specs=[
            pl.BlockSpec((1, MLSTM_QK, L), lambda bb, d, j: (bb, 0, cidx(d, j))),
            pl.BlockSpec((1, L, MLSTM_QK), lambda bb, d, j: (bb, cidx(d, j), 0)),
            pl.BlockSpec((1, D_MODEL, L), lambda bb, d, j: (bb, 0, cidx(d, j))),
            pl.BlockSpec((1, L, LANES), lambda bb, d, j: (bb, cidx(d, j), 0)),
            pl.BlockSpec((1, 4 * MLSTM_HEADS, L), lambda bb, d, j: (bb, 0, cidx(d, j))),
        ],
        out_specs=pl.BlockSpec((1, 1, D_MODEL, L), lambda bb, d, j: (d, bb, 0, cidx(d, j))),
        out_shape=jax.ShapeDtypeStruct((2, b, D_MODEL, n), F32),
        scratch_shapes=[
            pltpu.VMEM((MLSTM_HEADS, MLSTM_DV + LANES, MLSTM_DQK), F32),
            pltpu.VMEM((MLSTM_HEADS, SUBLANES, LANES), F32),
        ],
        compiler_params=_cparams(("parallel", "parallel", "arbitrary")),
        name="mlstm_scan",
    )(q_t, k, v_t, gcol, grow)


def _gqa_in_kernel(h_ref, g_ref, mod_ref, w_ref, qg_ref, kg_ref, cos_ref, sin_ref, q_ref, k_ref, v_ref):
    hd = ATTN_HEAD_DIM
    mod = mod_ref[0]
    u = _norm_mod(h_ref[0], g_ref[...], mod[0:1], mod[1:2]).astype(BF16)
    p = _dot(u, w_ref[...])
    cos = cos_ref[...]
    sin = sin_ref[...]
    lane = lax.broadcasted_iota(I32, cos.shape, 1)
    first_half = (lane & (hd // 4)) == 0

    def rope(xh):
        swapped = jnp.where(first_half, pltpu.roll(xh, hd - hd // 4, 1), pltpu.roll(xh, hd // 4, 1))
        return xh * cos + swapped * sin

    nq, nk = ATTN_HEADS * hd, ATTN_KV_HEADS * hd
    qn = _head_rms(p[:, :nq], hd)
    kn = _head_rms(p[:, nq:nq + nk], hd)
    qg = qg_ref[...]
    kg = kg_ref[...]
    for i in range(ATTN_HEADS):
        q_ref[0, :, i * hd:(i + 1) * hd] = (rope(qn[:, i * hd:(i + 1) * hd] * qg) * ATTN_Q_SCALE).astype(BF16)
    for i in range(ATTN_KV_HEADS):
        k_ref[0, :, i * hd:(i + 1) * hd] = rope(kn[:, i * hd:(i + 1) * hd] * kg).astype(BF16)
    v_ref[0] = p[:, nq + nk:].astype(BF16)


def _rope_tables(lay):
    hd = ATTN_HEAD_DIM
    quarter = hd // 4
    inv = ROPE_THETA ** (-jnp.arange(quarter, dtype=F32) / quarter)
    pos = jnp.arange(lay.n_lat)
    rows = (pos // GRID_W).astype(F32)
    cols = (pos % GRID_W).astype(F32)
    ang = jnp.concatenate([jnp.tile(rows[:, None] * inv, (1, 2)), jnp.tile(cols[:, None] * inv, (1, 2))], axis=1)
    sign = jnp.tile(jnp.concatenate([-jnp.ones(quarter, F32), jnp.ones(quarter, F32)]), 2)
    cos = jnp.concatenate([jnp.ones((lay.n_ctx, hd), F32), jnp.cos(ang)], axis=0)
    sin = jnp.concatenate([jnp.zeros((lay.n_ctx, hd), F32), jnp.sin(ang) * sign], axis=0)
    return cos, sin


def _gqa_in(h, norm_g, mod_l, w_in, q_g, k_g, lay):
    d, hd = D_MODEL, ATTN_HEAD_DIM
    nq, nk = ATTN_HEADS * hd, ATTN_KV_HEADS * hd
    cos, sin = _rope_tables(lay)
    b, n = lay.batch, lay.n
    tab_spec = pl.BlockSpec((TOKEN_TILE, hd), lambda bb, t: (t, 0))
    return pl.pallas_call(
        _gqa_in_kernel,
        grid=(b, lay.tiles),
        in_specs=[
            _tok_spec(d, lay), _full_spec((1, d)), _mod_spec(lay), _full_spec((d, nq + 2 * nk)),
            _full_spec((1, hd)), _full_spec((1, hd)), tab_spec, tab_spec,
        ],
        out_specs=[_tok_spec(nq, lay), _tok_spec(nk, lay), _tok_spec(nk, lay)],
        out_shape=[
            jax.ShapeDtypeStruct((b, n, nq), BF16),
            jax.ShapeDtypeStruct((b, n, nk), BF16),
            jax.ShapeDtypeStruct((b, n, nk), BF16),
        ],
        compiler_params=_cparams(("parallel", "parallel")),
        name="gqa_in",
    )(h, norm_g.reshape(1, d), mod_l, w_in.astype(BF16), q_g.reshape(1, hd), k_g.reshape(1, hd), cos, sin)


def _attn_kernel(q_ref, k_ref, v_ref, o_ref, *, ctx_tiles, n_ctx, n_all):
    hd = ATTN_HEAD_DIM
    t = pl.program_id(2)
    q = q_ref[0]
    rows = q.shape[0]
    q2 = jnp.concatenate([q[:, g * hd:(g + 1) * hd] for g in range(ATTN_GROUP)], axis=0)

    def attend(n_keys):
        bounds = [0, n_ctx] + list(range(n_ctx + ATTN_KV_CHUNK, n_keys + 1, ATTN_KV_CHUNK))
        assert bounds[-1] == n_keys
        m = l = acc = None
        for lo, hi in zip(bounds[:-1], bounds[1:]):
            s = _dot_nt(q2, k_ref[0, lo:hi, :])
            mc = jnp.max(s, axis=-1, keepdims=True)
            if m is None:
                m_new = mc
                p = jnp.exp2(s - m_new)
                l = jnp.sum(p, axis=-1, keepdims=True)
                acc = _dot(p.astype(BF16), v_ref[0, lo:hi, :])
            else:
                m_new = jnp.maximum(m, mc)
                alpha = jnp.exp2(m - m_new)
                p = jnp.exp2(s - m_new)
                l = alpha * l + jnp.sum(p, axis=-1, keepdims=True)
                acc = alpha * acc + _dot(p.astype(BF16), v_ref[0, lo:hi, :])
            m = m_new
        o = acc / l
        for g in range(ATTN_GROUP):
            o_ref[0, :, g * hd:(g + 1) * hd] = o[g * rows:(g + 1) * rows].astype(BF16)

    @pl.when(t < ctx_tiles)
    def _():
        attend(n_ctx)

    @pl.when(t >= ctx_tiles)
    def _():
        attend(n_all)


def _attention(q, k, v, lay):
    hd = ATTN_HEAD_DIM
    b, n = lay.batch, lay.n
    gw = ATTN_GROUP * hd
    kern = functools.partial(_attn_kernel, ctx_tiles=lay.ctx_tiles, n_ctx=lay.n_ctx, n_all=n)
    return pl.pallas_call(
        kern,
        grid=(b, ATTN_KV_HEADS, lay.tiles),
        in_specs=[
            pl.BlockSpec((1, TOKEN_TILE, gw), lambda bb, kh, t: (bb, t, kh)),
            pl.BlockSpec((1, n, hd), lambda bb, kh, t: (bb, 0, kh)),
            pl.BlockSpec((1, n, hd), lambda bb, kh, t: (bb, 0, kh)),
        ],
        out_specs=pl.BlockSpec((1, TOKEN_TILE, gw), lambda bb, kh, t: (bb, t, kh)),
        out_shape=jax.ShapeDtypeStruct((b, n, ATTN_HEADS * hd), BF16),
        compiler_params=_cparams(("parallel", "parallel", "parallel")),
        name="attention",
    )(q, k, v)


def _gla_in_kernel(h_ref, g_ref, mod_ref, w_ref, wa_ref, aw_ref, ab_ref,
                   q_ref, k_ref, v_ref, gate_ref, laf_ref, lab_ref):
    d = D_MODEL
    mod = mod_ref[0]
    u = _norm_mod(h_ref[0], g_ref[...], mod[0:1], mod[1:2]).astype(BF16)
    p = _dot(u, w_ref[...])
    q_ref[0] = p[:, :GLA_QK] * GLA_DK ** -0.5
    k_ref[0] = p[:, GLA_QK:2 * GLA_QK]
    v_ref[0] = p[:, 2 * GLA_QK:2 * GLA_QK + d].astype(BF16)
    gate_ref[0] = p[:, 2 * GLA_QK + d:]
    a_low = _dot(u, wa_ref[...]).astype(BF16)
    pre = _dot(a_low, aw_ref[...]) + ab_ref[...]
    la = _log_sigmoid(pre) * (1.0 / GLA_TAU)
    laf_ref[0] = la[:, :GLA_QK]
    lab_ref[0] = la[:, GLA_QK:]


def _gla_in(h, norm_g, mod_l, w_in, alpha_w2, alpha_b, lay):
    d, r = D_MODEL, GLA_GATE_RANK
    n_main = 2 * GLA_QK + 2 * d
    w_main = w_in[:, :n_main].astype(BF16)
    wa = jnp.zeros((d, LANES), F32).at[:, :2 * r].set(w_in[:, n_main:]).astype(BF16)
    aw = jnp.zeros((LANES, 2 * GLA_QK), F32)
    aw = aw.at[:r, :GLA_QK].set(alpha_w2[0]).at[r:2 * r, GLA_QK:].set(alpha_w2[1]).astype(BF16)
    ab = alpha_b.reshape(1, 2 * GLA_QK)
    b, n = lay.batch, lay.n
    return pl.pallas_call(
        _gla_in_kernel,
        grid=(b, lay.tiles),
        in_specs=[
            _tok_spec(d, lay), _full_spec((1, d)), _mod_spec(lay), _full_spec((d, n_main)),
            _full_spec((d, LANES)), _full_spec((LANES, 2 * GLA_QK)), _full_spec((1, 2 * GLA_QK)),
        ],
        out_specs=[_tok_spec(GLA_QK, lay), _tok_spec(GLA_QK, lay), _tok_spec(d, lay), _tok_spec(d, lay),
                   _tok_spec(GLA_QK, lay), _tok_spec(GLA_QK, lay)],
        out_shape=[
            jax.ShapeDtypeStruct((b, n, GLA_QK), F32),
            jax.ShapeDtypeStruct((b, n, GLA_QK), F32),
            jax.ShapeDtypeStruct((b, n, d), BF16),
            jax.ShapeDtypeStruct((b, n, d), F32),
            jax.ShapeDtypeStruct((b, n, GLA_QK), F32),
            jax.ShapeDtypeStruct((b, n, GLA_QK), F32),
        ],
        compiler_params=_cparams(("parallel", "parallel")),
        name="gla_in",
    )(h, norm_g.reshape(1, d), mod_l, w_main, wa, aw, ab)


def _gla_scan_kernel(q_ref, k_ref, v_ref, la_ref, out_ref, st_ref, *, reverse):
    T, C = GLA_TILE, GLA_CHUNK
    dk, dv, nh = GLA_DK, GLA_DV, GLA_HEADS
    j = pl.program_id(1)

    @pl.when(j == 0)
    def _():
        st_ref[...] = jnp.zeros_like(st_ref)

    n_chunks = T // C
    row = lax.broadcasted_iota(I32, (T, T), 0)
    col = lax.broadcasted_iota(I32, (T, T), 1)
    before = (col >= row) if reverse else (col <= row)
    b = jnp.dot(before.astype(F32), la_ref[0], precision=HIGHEST, preferred_element_type=F32)
    b_end = b[0:1] if reverse else b[T - 1:T]
    q = q_ref[0]
    k = k_ref[0]
    q_state = (q * jnp.exp(b)).astype(BF16)
    k_state = (k * jnp.exp(b_end - b)).astype(BF16)
    decay = jnp.exp(b_end)

    def rows_of(c):
        return slice(c * C, (c + 1) * C)

    b_mid = jnp.concatenate(
        [jnp.broadcast_to(b[c * C + C // 2:c * C + C // 2 + 1], (C, b.shape[1])) for c in range(n_chunks)], axis=0)
    q_mid = (q * jnp.exp(b - b_mid)).astype(BF16)
    k_mid = (k * jnp.exp(b_mid - b)).astype(BF16)
    same_chunk = (row // C) == (col // C)
    scan_order = list(reversed(range(n_chunks))) if reverse else list(range(n_chunks))
    later = scan_order[1:]
    q_off, k_off, off_mask = {}, {}, {}
    for c in later:
        edge = (c + 1) * C if reverse else c * C - 1
        b_edge = b[edge:edge + 1]
        q_off[c] = (q[rows_of(c)] * jnp.exp(b[rows_of(c)] - b_edge)).astype(BF16)
        k_off[c] = (k * jnp.exp(jnp.minimum(b_edge - b, 0.0))).astype(BF16)
        ccol = lax.broadcasted_iota(I32, (C, T), 1)
        off_mask[c] = (ccol >= (c + 1) * C) if reverse else (ccol < c * C)

    for h in range(nh):
        ks = slice(h * dk, (h + 1) * dk)
        vs = slice(h * dv, (h + 1) * dv)
        v_h = v_ref[0, :, vs]
        a_same = jnp.where(same_chunk & before, _dot_nt(q_mid[:, ks], k_mid[:, ks]), 0.0)
        blocks = []
        for c in range(n_chunks):
            a_c = a_same[rows_of(c)]
            if c in q_off:
                a_c = a_c + jnp.where(off_mask[c], _dot_nt(q_off[c][:, ks], k_off[c][:, ks]), 0.0)
            blocks.append(a_c)
        a = jnp.concatenate(blocks, axis=0).astype(BF16)
        st = st_ref[h]
        out_ref[0, :, vs] = _dot_nt(q_state[:, ks], st.astype(BF16)) + _dot(a, v_h)
        st_ref[h] = st * decay[:, ks] + _dot_tn(v_h, k_state[:, ks])


def _gla_scan(q, k, v, la, lay, reverse):
    b, n = lay.batch, lay.n
    n_tiles, ctx_tiles = n // GLA_TILE, lay.n_ctx // GLA_TILE

    def tidx(j):
        return _scan_chunk_index(reverse, j, n_tiles, ctx_tiles)

    kern = functools.partial(_gla_scan_kernel, reverse=reverse)
    return pl.pallas_call(
        kern,
        grid=(b, n_tiles),
        in_specs=[
            pl.BlockSpec((1, GLA_TILE, GLA_QK), lambda bb, j: (bb, tidx(j), 0)),
            pl.BlockSpec((1, GLA_TILE, GLA_QK), lambda bb, j: (bb, tidx(j), 0)),
            pl.BlockSpec((1, GLA_TILE, D_MODEL), lambda bb, j: (bb, tidx(j), 0)),
            pl.BlockSpec((1, GLA_TILE, GLA_QK), lambda bb, j: (bb, tidx(j), 0)),
        ],
        out_specs=pl.BlockSpec((1, GLA_TILE, D_MODEL), lambda bb, j: (bb, tidx(j), 0)),
        out_shape=jax.ShapeDtypeStruct((b, n, D_MODEL), F32),
        scratch_shapes=[pltpu.VMEM((GLA_HEADS, GLA_DV, GLA_DK), F32)],
        compiler_params=_cparams(("parallel", "arbitrary")),
        name="gla_scan_bwd" if reverse else "gla_scan_fwd",
    )(q, k, v, la)


def _readout_kernel(*refs, mode, first_routed_tile):
    if mode == "attn":
        a_ref, wout_ref = refs[:2]
        rest = refs[2:]
        y_in = a_ref[0]
    else:
        a_ref, b_ref, gsrc_ref, ng_ref, wout_ref = refs[:5]
        rest = refs[5:]
        hs = a_ref[0] + b_ref[0]
        if mode == "mlstm":
            normed = []
            for i in range(MLSTM_HEADS):
                xs = hs[i * MLSTM_DV:(i + 1) * MLSTM_DV, :]
                ms = jnp.mean(xs * xs, axis=0, keepdims=True)
                normed.append(xs * lax.rsqrt(ms + NORM_EPS))
            y_in = (jnp.concatenate(normed, axis=0) * ng_ref[...] * _sigmoid(gsrc_ref[0])).T
        else:
            gsrc = gsrc_ref[0]
            y_in = _head_rms(hs, GLA_DV) * ng_ref[...] * (gsrc * _sigmoid(gsrc))
        y_in = y_in.astype(BF16)
    (h_ref, mod_ref, n2g_ref, wr_ref, br_ref,
     hnew_ref, v_ref, e_ref, gate_ref, rank_ref, cnt_ref, carry_ref) = rest

    first = (pl.program_id(0) == 0) & (pl.program_id(1) == 0)

    @pl.when(first)
    def _():
        carry_ref[...] = jnp.zeros_like(carry_ref)

    mod = mod_ref[0]
    hn = h_ref[0] + mod[2:3] * _dot(y_in, wout_ref[...])
    hnew_ref[0] = hn
    v = _norm_mod(hn, n2g_ref[...], mod[3:4], mod[4:5])
    for c in range(ROW_TILE):
        v_ref[0, pl.ds(c, v.shape[0], stride=ROW_TILE), :] = v[:, c * LANES:(c + 1) * LANES]

    logits = _dot(v.astype(BF16), wr_ref[...]) + br_ref[...]

    tm = logits.shape[0]
    lane = lax.broadcasted_iota(I32, logits.shape, 1)
    lane_f = lane.astype(F32)
    work = logits
    top_e = jnp.zeros(logits.shape, F32)
    top_p = jnp.zeros(logits.shape, F32)
    onehot = jnp.zeros(logits.shape, F32)
    m0 = None
    hits = []
    for jx in range(TOP_K):
        mx = jnp.max(work, axis=-1, keepdims=True)
        idx = jnp.min(jnp.where(work == mx, lane_f, float(LANES)), axis=-1, keepdims=True)
        if jx == 0:
            m0 = mx
        hit = lane_f == idx
        hits.append(hit)
        top_e = jnp.where(lane == jx, idx, top_e)
        top_p = jnp.where(lane == jx, jnp.exp(mx - m0), top_p)
        onehot = jnp.where(hit, 1.0, onehot)
        work = jnp.where(hit, -jnp.inf, work)
    e_ref[0] = top_e.astype(I32)
    gate_ref[0] = top_p / jnp.sum(top_p, axis=-1, keepdims=True)
    if first_routed_tile:
        onehot = onehot * (pl.program_id(1) >= first_routed_tile).astype(F32)

    r = lax.broadcasted_iota(I32, (tm, tm), 0)
    c = lax.broadcasted_iota(I32, (tm, tm), 1)
    tril = (c <= r).astype(BF16)
    cum = _dot(tril, onehot.astype(BF16))
    carry = carry_ref[0:1, :]
    rank_all = carry + cum - onehot
    rank_sel = jnp.zeros(logits.shape, F32)
    for jx in range(TOP_K):
        picked = jnp.sum(jnp.where(hits[jx], rank_all, 0.0), axis=-1, keepdims=True)
        rank_sel = jnp.where(lane == jx, picked, rank_sel)
    rank_ref[0] = rank_sel.astype(I32)
    total = carry + cum[tm - 1:tm, :]
    carry_ref[...] = jnp.broadcast_to(total, carry_ref.shape)
    cnt_ref[...] = jnp.broadcast_to(total, cnt_ref.shape)


def _readout(mode, mixer_outs, w_out, h, mod_l, norm2_g, router_w, router_b, lay, latent_only):
    d = D_MODEL
    b, n = lay.batch, lay.n
    wr = jnp.zeros((d, LANES), F32).at[:, :N_EXPERTS].set(router_w).astype(BF16)
    br =jnp.full((1, LANES), NEG_BIG, F32).at[0, :N_EXPERTS].set(router_b)
    if mode == "attn":
        (attn_o,) = mixer_outs
        head_in = [attn_o]
        head_specs = [_tok_spec(d, lay)]
    elif mode == "mlstm":
        h_dir, o_gate, norm_g = mixer_outs
        head_in = [h_dir, h_dir, o_gate, norm_g.reshape(d, 1)]
        head_specs = [pl.BlockSpec((None, 1, d, TOKEN_TILE), lambda bb, t: (0, bb, 0, t)),
                      pl.BlockSpec((None, 1, d, TOKEN_TILE), lambda bb, t: (1, bb, 0, t)),
                      _feat_spec(d), _full_spec((d, 1))]
    else:
        o_f, o_b, g_gate, norm_g = mixer_outs
        head_in = [o_f, o_b, g_gate, norm_g.reshape(1, d)]
        head_specs = [_tok_spec(d, lay), _tok_spec(d, lay), _tok_spec(d, lay), _full_spec((1, d))]
    kern = functools.partial(_readout_kernel, mode=mode, first_routed_tile=lay.ctx_tiles if latent_only else 0)
    return pl.pallas_call(
        kern,
        grid=(b, lay.tiles),
        in_specs=head_specs + [
            _full_spec((d, d)), _tok_spec(d, lay), _mod_spec(lay), _full_spec((1, d)),
            _full_spec((d, LANES)), _full_spec((1, LANES)),
        ],
        out_specs=[_tok_spec(d, lay), pl.BlockSpec((1, TOKEN_TILE * ROW_TILE, LANES), lambda bb, t: (bb, t, 0)),
                   _tok_spec(LANES, lay), _tok_spec(LANES, lay),
                   _tok_spec(LANES, lay), _full_spec((SUBLANES, LANES))],
        out_shape=[
            jax.ShapeDtypeStruct((b, n, d), F32),
            jax.ShapeDtypeStruct((b, n * ROW_TILE, LANES), F32),
            jax.ShapeDtypeStruct((b, n, LANES), I32),
            jax.ShapeDtypeStruct((b, n, LANES), F32),
            jax.ShapeDtypeStruct((b, n, LANES), I32),
            jax.ShapeDtypeStruct((SUBLANES, LANES), F32),
        ],
        scratch_shapes=[pltpu.VMEM((SUBLANES, LANES), F32)],
        compiler_params=_cparams(("arbitrary", "arbitrary")),
        name="readout_" + mode,
    )(*head_in, w_out.astype(BF16), h, mod_l, norm2_g.reshape(1, d), wr, br)


def _slot(ps_ref, e_ref, r_ref, i):
    return ps_ref[e_ref[0, 0, i]] + r_ref[0, 0, i]


def _row_copies_wait(src_block, dst_rows, sem):
    for _ in range(TOP_K):
        pltpu.make_async_copy(src_block, dst_rows, sem).wait()


def _zero_fill(zs_ref, nu_ref, xb_ref, zbuf_ref, sem):
    zbuf_ref[...] = jnp.zeros_like(zbuf_ref)
    blk_rows = zbuf_ref.shape[0]
    n_blocks = xb_ref.shape[0] // blk_rows

    def zero_block(start):
        return pltpu.make_async_copy(zbuf_ref, xb_ref.at[pl.ds(start, blk_rows), :], sem)

    def trailing(i, then):
        pl.when(i >= nu_ref[0])(lambda: then(zero_block(i * blk_rows)))

    for e in range(N_EXPERTS):
        zero_block(pl.multiple_of(zs_ref[e] * ROW_TILE, ROW_TILE)).start()
    for e in range(N_EXPERTS):
        zero_block(0).wait()
    for i in range(n_blocks - N_EXPERTS, n_blocks):
        trailing(i, lambda cp: cp.start())
    for i in range(n_blocks - N_EXPERTS, n_blocks):
        trailing(i, lambda cp: cp.wait())


def _dispatch_kernel(e_ref, r_ref, ps_ref, zs_ref, nu_ref, v_ref, xb_ref, zbuf_ref, sem, *, first_routed_tile):
    rows = v_ref.shape[1] // ROW_TILE

    @pl.when((pl.program_id(0) == 0) & (pl.program_id(1) == 0))
    def _():
        _zero_fill(zs_ref, nu_ref, xb_ref, zbuf_ref, sem)

    def body(r, carry):
        src = v_ref.at[0, pl.ds(pl.multiple_of(r * ROW_TILE, ROW_TILE), ROW_TILE), :]
        for jx in range(TOP_K):
            slot = _slot(ps_ref, e_ref, r_ref, r * TOP_K + jx)
            dst = xb_ref.at[pl.ds(pl.multiple_of(slot * ROW_TILE, ROW_TILE), ROW_TILE), :]
            pltpu.make_async_copy(src, dst, sem).start(priority=jx % DMA_THREADS)
        return carry

    @pl.when(pl.program_id(1) >= first_routed_tile)
    def _():
        lax.fori_loop(0, rows, body, 0, unroll=DMA_UNROLL)
        _row_copies_wait(v_ref.at[0], xb_ref.at[pl.ds(0, rows * ROW_TILE), :], sem)


def _assign_spec(lay):
    tiles = lay.tiles
    return pl.BlockSpec((1, 1, TOKEN_TILE * TOP_K), lambda bb, t: (bb * tiles + t, 0, 0), memory_space=pltpu.SMEM)


def _dispatch(e_c, r_c, pad_starts, zero_start, n_used, v_rows, cap, lay, latent_only):
    return pl.pallas_call(
        functools.partial(_dispatch_kernel, first_routed_tile=lay.ctx_tiles if latent_only else 0),
        grid=(lay.batch, lay.tiles),
        in_specs=[
            _assign_spec(lay), _assign_spec(lay),
            pl.BlockSpec(memory_space=pltpu.SMEM),
            pl.BlockSpec(memory_space=pltpu.SMEM),
            pl.BlockSpec(memory_space=pltpu.SMEM),
            pl.BlockSpec((1, TOKEN_TILE * ROW_TILE, LANES), lambda bb, t: (bb, t, 0)),
        ],
        out_specs=pl.BlockSpec(memory_space=pl.ANY),
        out_shape=jax.ShapeDtypeStruct((cap * ROW_TILE, LANES), F32),
        scratch_shapes=[pltpu.VMEM((MOE_BLOCK * ROW_TILE, LANES), F32), pltpu.SemaphoreType.DMA(())],
        compiler_params=_cparams(("arbitrary", "arbitrary")),
        name="moe_dispatch",
    )(e_c, r_c, pad_starts, zero_start, n_used, v_rows)


def _expert_kernel(be_ref, nu_ref, x_ref, wgu_ref, bgu_ref, wd_ref, bd_ref, y_ref, wgu_bf_ref, wd_bf_ref):
    i = pl.program_id(0)
    e = be_ref[i]
    used = i < nu_ref[0]
    new_expert = (i == 0) | (e != be_ref[jnp.maximum(i - 1, 0)])

    @pl.when(new_expert)
    def _():
        wgu_bf_ref[...] = wgu_ref[0, 0].astype(BF16)
        wd_bf_ref[...] = wd_ref[0, 0].astype(BF16)

    @pl.when(jnp.logical_not(used))
    def _():
        y_ref[...] = jnp.zeros_like(y_ref)

    @pl.when(used)
    def _():
        x = jnp.concatenate([x_ref[pl.ds(c, MOE_BLOCK, stride=ROW_TILE), :] for c in range(ROW_TILE)], axis=1)
        gu = _dot(x.astype(BF16), wgu_bf_ref[...]) + bgu_ref[0, 0]
        g = jnp.minimum(gu[:, :D_EXPERT], SWIGLU_LIMIT)
        lin = jnp.clip(gu[:, D_EXPERT:], -SWIGLU_LIMIT, SWIGLU_LIMIT)
        hdn = g * _sigmoid(SWIGLU_ALPHA * g) * (lin + 1.0)
        y = _dot(hdn.astype(BF16), wd_bf_ref[...]) + bd_ref[0, 0]
        for c in range(ROW_TILE):
            y_ref[pl.ds(c, MOE_BLOCK, stride=ROW_TILE), :] = y[:, c * LANES:(c + 1) * LANES]


def _experts(block_e, n_used, xb, w_gu, b_gu, w_d, b_d, layer):
    d, de = D_MODEL, D_EXPERT
    depth = w_gu.shape[0]
    blk_rows = MOE_BLOCK * ROW_TILE
    n_blocks = xb.shape[0] // blk_rows

    def row_map(i, be, nu):
        return (jnp.minimum(i, nu[0] - 1), 0)

    def exp_map(i, be, nu):
        return (layer, be[i], 0, 0)

    grid_spec = pltpu.PrefetchScalarGridSpec(
        num_scalar_prefetch=2,
        grid=(n_blocks,),
        in_specs=[
            pl.BlockSpec((blk_rows, LANES), row_map),
            pl.BlockSpec((1, 1, d, 2 * de), exp_map),
            pl.BlockSpec((1, 1, 1, 2 * de), exp_map),
            pl.BlockSpec((1, 1, de, d), exp_map),
            pl.BlockSpec((1, 1, 1, d), exp_map),
        ],
        out_specs=pl.BlockSpec((blk_rows, LANES), lambda i, be, nu: (i, 0)),
        scratch_shapes=[pltpu.VMEM((d, 2 * de), BF16), pltpu.VMEM((de, d), BF16)],
    )
    return pl.pallas_call(
        _expert_kernel,
        grid_spec=grid_spec,
        out_shape=jax.ShapeDtypeStruct(xb.shape, F32),
        compiler_params=_cparams(("arbitrary",), vmem=EXPERT_VMEM_LIMIT),
        name="moe_experts",
    )(block_e, n_used, xb, w_gu, b_gu.reshape(depth, N_EXPERTS, 1, 2 * de), w_d,
      b_d.reshape(depth, N_EXPERTS, 1, d))


def _combine_kernel(e_ref, r_ref, en_ref, rn_ref, gate_ref, ps_ref, yb_ref, h_ref, mod_ref, out_ref,
                    ybuf_ref, acc_ref, sems, *, tiles_per_sample, n_tiles):
    rows = h_ref.shape[1]
    tile = pl.program_id(0) * tiles_per_sample + pl.program_id(1)
    buf = tile % 2

    def issue_rows(e_src, r_src, into):
        def issue(r, carry):
            base = pl.multiple_of(r * ROW_TILE, ROW_TILE)
            for jx in range(TOP_K):
                slot = _slot(ps_ref, e_src, r_src, r * TOP_K + jx)
                src = yb_ref.at[pl.ds(pl.multiple_of(slot * ROW_TILE, ROW_TILE), ROW_TILE), :]
                pltpu.make_async_copy(src, ybuf_ref.at[into, jx, pl.ds(base, ROW_TILE), :], sems.at[into]).start(
                    priority=jx % DMA_THREADS)
            return carry

        lax.fori_loop(0, rows, issue, 0, unroll=DMA_UNROLL)

    @pl.when(tile == 0)
    def _():
        issue_rows(e_ref, r_ref, 0)

    @pl.when(tile + 1 < n_tiles)
    def _():
        issue_rows(en_ref, rn_ref, 1 - buf)

    _row_copies_wait(yb_ref.at[pl.ds(0, rows * ROW_TILE), :], ybuf_ref.at[buf, 0], sems.at[buf])

    def accumulate(r, carry):
        base = pl.multiple_of(r * ROW_TILE, ROW_TILE)
        a = gate_ref[0, 0, r * TOP_K] * ybuf_ref[buf, 0, pl.ds(base, ROW_TILE), :]
        for jx in range(1, TOP_K):
            a = a + gate_ref[0, 0, r * TOP_K + jx] * ybuf_ref[buf, jx, pl.ds(base, ROW_TILE), :]
        acc_ref[pl.ds(base, ROW_TILE), :] = a
        return carry

    lax.fori_loop(0, rows, accumulate, 0, unroll=DMA_UNROLL)
    for c in range(ROW_TILE):
        cs = slice(c * LANES, (c + 1) * LANES)
        out_ref[0, :, cs] = h_ref[0, :, cs] + mod_ref[0, 5:6, cs] * acc_ref[pl.ds(c, rows, stride=ROW_TILE), :]


def _combine(e_c, r_c, gate_c, pad_starts, yb, h, mod_l, lay, latent_only):
    d = D_MODEL
    tiles = lay.tiles
    n_tiles = lay.batch * tiles
    if latent_only:
        out_spec = pl.BlockSpec((1, TOKEN_TILE, d), lambda bb, t: (bb, jnp.maximum(t - lay.ctx_tiles, 0), 0))
        out_rows = lay.n_lat
    else:
        out_spec = _tok_spec(d, lay)
        out_rows = lay.n
    next_spec = pl.BlockSpec((1, 1, TOKEN_TILE * TOP_K),
                             lambda bb, t: (jnp.minimum(bb * tiles + t + 1, n_tiles - 1), 0, 0),
                             memory_space=pltpu.SMEM)
    return pl.pallas_call(
        functools.partial(_combine_kernel, tiles_per_sample=tiles, n_tiles=n_tiles),
        grid=(lay.batch, tiles),
        in_specs=[_assign_spec(lay), _assign_spec(lay), next_spec, next_spec, _assign_spec(lay),
                  pl.BlockSpec(memory_space=pltpu.SMEM),
                  pl.BlockSpec(memory_space=pl.ANY), _tok_spec(d, lay), _mod_spec(lay)],
        out_specs=out_spec,
        out_shape=jax.ShapeDtypeStruct((lay.batch, out_rows, d), F32),
        scratch_shapes=[pltpu.VMEM((2, TOP_K, TOKEN_TILE * ROW_TILE, LANES), F32),
                        pltpu.VMEM((TOKEN_TILE * ROW_TILE, LANES), F32), pltpu.SemaphoreType.DMA((2,))],
        compiler_params=_cparams(("arbitrary", "arbitrary")),
        name="moe_combine",
    )(e_c, r_c, e_c, r_c, gate_c, pad_starts, yb, h, mod_l)


def _moe(h, v_rows, top_e, gate, rank, counts, mod_l, w_gu, b_gu, w_d, b_d, layer, lay, latent_only):
    n_tokens = lay.batch * lay.n
    n_assign = n_tokens * TOP_K
    cap = (-(-n_assign // MOE_BLOCK) + N_EXPERTS) * MOE_BLOCK
    n_blocks = cap // MOE_BLOCK
    cnt = counts[0, :N_EXPERTS].astype(I32)
    padded = (cnt + MOE_BLOCK - 1) // MOE_BLOCK * MOE_BLOCK
    pad_ends = jnp.cumsum(padded)
    pad_starts = pad_ends - padded
    n_used = (pad_ends[-1] // MOE_BLOCK).astype(I32).reshape(1)
    blk = jnp.minimum(jnp.arange(n_blocks, dtype=I32), n_used[0] - 1) * MOE_BLOCK
    block_e = jnp.minimum(jnp.sum((pad_ends[None, :] <= blk[:, None]).astype(I32), axis=1), N_EXPERTS - 1)
    pad_starts = pad_starts.astype(I32)
    zero_start = jnp.minimum(pad_starts + cnt, cap - MOE_BLOCK).astype(I32)

    def per_assignment(a):
        return a[:, :, :TOP_K].reshape(lay.batch * lay.tiles, 1, TOKEN_TILE * TOP_K)

    e_c, r_c, gate_c = per_assignment(top_e), per_assignment(rank), per_assignment(gate)
    xb = _dispatch(e_c, r_c, pad_starts, zero_start, n_used, v_rows, cap, lay, latent_only)
    yb = _experts(block_e, n_used, xb, w_gu, b_gu, w_d, b_d, layer)
    return _combine(e_c, r_c, gate_c, pad_starts, yb, h, mod_l, lay, latent_only)


def kernel(x, c, ctx, c_ctx, norm1_g, norm2_g, mod_w, mod_b,
           mlstm_w_in, mlstm_conv_w, mlstm_gate_b, mlstm_out_norm_g, mlstm_w_out,
           attn_w_in, attn_q_norm_g, attn_k_norm_g, attn_w_out,
           gla_w_in, gla_alpha_w2, gla_alpha_b, gla_out_norm_g, gla_w_out,
           router_w, router_b, moe_w_gu, moe_b_gu, moe_w_down, moe_b_down):
    batch, n_lat, d = x.shape
    n_ctx = ctx.shape[1]
    depth = norm1_g.shape[0]
    assert d == D_MODEL
    lay = _Layout(batch, n_ctx, n_lat)

    cond_rows = -(-(batch + 1) // SUBLANES) * SUBLANES
    cond = jnp.zeros((cond_rows, d), F32).at[:batch].set(c).at[batch].set(c_ctx)
    mod_all = _modulation(cond, mod_w, mod_b)

    h = jnp.concatenate([ctx, x], axis=1)
    for layer in range(depth):
        kind, j = layer % 3, layer // 3
        last = layer == depth - 1
        mod_l = mod_all[layer]
        if kind == 0:
            qk_pre, v_t, o_gate, gcol, grow = _mlstm_in(h, norm1_g[layer], mod_l, mlstm_w_in[j],
                                                        mlstm_gate_b[j], lay)
            q_t, k_m = _mlstm_conv(qk_pre, mlstm_conv_w[j], lay)
            h_dir = _mlstm_scan(q_t, k_m, v_t, gcol, grow, lay)
            mixer = ("mlstm", (h_dir, o_gate, mlstm_out_norm_g[j]), mlstm_w_out[j])
        elif kind == 1:
            q_a, k_a, v_a = _gqa_in(h, norm1_g[layer], mod_l, attn_w_in[j], attn_q_norm_g[j],
                                    attn_k_norm_g[j], lay)
            mixer = ("attn", (_attention(q_a, k_a, v_a, lay),), attn_w_out[j])
        else:
            q_g, k_g, v_g, g_gate, la_f, la_b = _gla_in(h, norm1_g[layer], mod_l, gla_w_in[j],
                                                        gla_alpha_w2[j], gla_alpha_b[j], lay)
            o_f = _gla_scan(q_g, k_g, v_g, la_f, lay, reverse=False)
            o_b = _gla_scan(q_g, k_g, v_g, la_b, lay, reverse=True)
            mixer = ("gla", (o_f, o_b, g_gate, gla_out_norm_g[j]), gla_w_out[j])
        mode, mixer_outs, w_out = mixer
        h, v_moe, top_e, gate, rank, counts = _readout(mode, mixer_outs, w_out, h, mod_l, norm2_g[layer],
                                                       router_w[layer], router_b[layer], lay, last)
        h = _moe(h, v_moe, top_e, gate, rank, counts, mod_l,
                 moe_w_gu, moe_b_gu, moe_w_down, moe_b_down, layer, lay, latent_only=last)
    return h
```
